```python
import jax
import jax.numpy as jnp
from jax import lax
import numpy as np

D_MODEL = 1024
BATCH = 32
SEQ = 2048
DEPTH = 1

RW_HEAD_DIM = 64
RW_WIDTH = D_MODEL // 2
RW_HEADS = RW_WIDTH // RW_HEAD_DIM
RW_DECAY_RANK = 64
RW_ICLR_RANK = 64
RW_GATE_RANK = 128
RW_GN_EPS = 64e-5
RET_QK_DIM = 64
RET_QK_WIDTH = D_MODEL // 2
RET_HEADS = RET_QK_WIDTH // RET_QK_DIM
RET_V_DIM = 2 * RET_QK_DIM
RET_V_WIDTH = RET_HEADS * RET_V_DIM
RET_CHUNK = 128
ROPE_BASE = 10000.0
RET_GN_EPS = 1e-5
D_FF = 2816
CONV_WIDTH = 3
RMS_EPS = 1e-6

RW_SIZES = (RW_WIDTH, RW_WIDTH, RW_WIDTH, RW_DECAY_RANK, RW_ICLR_RANK, RW_GATE_RANK)
RET_SIZES = (RET_QK_WIDTH, RET_QK_WIDTH, RET_V_WIDTH, RET_V_WIDTH)
RW_COLS = sum(RW_SIZES)
RET_COLS = sum(RET_SIZES)
IN_SIZES = (RW_COLS, RET_COLS, D_MODEL, D_MODEL)
IN_COLS = sum(IN_SIZES)

kernel_name = "hybrid_rwkv7_retention_convffn_block"


def _split(z, sizes):
    return jnp.split(z, [int(s) for s in np.cumsum(sizes)[:-1]], axis=-1)


def rms_norm(x, g):
    xf = x.astype(jnp.float32)
    y = xf * lax.rsqrt(jnp.mean(xf * xf, axis=-1, keepdims=True) + RMS_EPS)
    return (y * g.astype(jnp.float32)).astype(x.dtype)


def group_norm_heads(z, eps):
    mu = jnp.mean(z, axis=-1, keepdims=True)
    var = jnp.mean(jnp.square(z - mu), axis=-1, keepdims=True)
    return (z - mu) * lax.rsqrt(var + eps)


def token_shift(p):
    return jnp.pad(p, ((0, 0), (1, 0), (0, 0)))[:, :-1]


def rwkv7_mixer(p, mu, w0, w2, a0, a2, g2, k_k, k_a, r_k, lnx_w, lnx_b):
    B, T, _ = p.shape
    H, N = RW_HEADS, RW_HEAD_DIM
    p = p.astype(jnp.float32)
    p = p + (token_shift(p) - p) * mu.astype(jnp.float32)
    r, k, v, wd, ad, gd = _split(p, RW_SIZES)
    w = -jax.nn.softplus(-(w0 + jnp.tanh(wd) @ w2)) - 0.5
    decay = jnp.exp(-jnp.exp(w))
    a = jax.nn.sigmoid(a0 + ad @ a2)
    g = jax.nn.sigmoid(gd) @ g2
    heads = lambda z: z.reshape(B, T, H, N).astype(jnp.float32)
    kk = heads(k * k_k)
    kk = kk / jnp.maximum(jnp.sqrt(jnp.sum(kk * kk, axis=-1, keepdims=True)), 1e-12)
    k = k * (1.0 + (a - 1.0) * k_a)
    r, k, v, decay, a = heads(r), heads(k), heads(v), heads(decay), heads(a)

    def step(S, inp):
        r_t, k_t, v_t, w_t, kk_t, a_t = inp
        sa = jnp.einsum('bhvk,bhk->bhv', S, -kk_t)
        S = (S * w_t[:, :, None, :]
             + sa[..., None] * (kk_t * a_t)[:, :, None, :]
             + v_t[..., None] * k_t[:, :, None, :])
        y = jnp.einsum('bhvk,bhk->bhv', S, r_t)
        return S, y

    xs = tuple(jnp.moveaxis(z, 1, 0) for z in (r, k, v, decay, kk, a))
    S0 = jnp.zeros((B, H, N, N), jnp.float32)
    _, y = lax.scan(step, S0, xs)
    y = jnp.moveaxis(y, 0, 1)
    y = group_norm_heads(y, RW_GN_EPS) * lnx_w.reshape(H, N) + lnx_b.reshape(H, N)
    bonus = jnp.sum(r * k * r_k.reshape(H, N), axis=-1, keepdims=True) * v
    return (y + bonus).reshape(B, T, RW_WIDTH) * g


def rotary(z, pos):
    half = z.shape[-1] // 2
    inv = ROPE_BASE ** (-jnp.arange(half, dtype=jnp.float32) / half)
    ang = pos.astype(jnp.float32)[..., None] * inv
    cos = jnp.cos(ang)[:, :, None, :]
    sin = jnp.sin(ang)[:, :, None, :]
    z1, z2 = z[..., :half], z[..., half:]
    return jnp.concatenate([z1 * cos - z2 * sin, z1 * sin + z2 * cos], axis=-1)


def retention_mixer(p, positions):
    B, T, _ = p.shape
    H, dk, dv, C = RET_HEADS, RET_QK_DIM, RET_V_DIM, RET_CHUNK
    q, k, v, g = _split(p.astype(jnp.float32), RET_SIZES)
    q = rotary(q.reshape(B, T, H, dk), positions)
    k = rotary(k.reshape(B, T, H, dk), positions) * (dk ** -0.5)
    v = v.reshape(B, T, H, dv)
    log_gamma = jnp.log1p(-jnp.exp2(-5.0 - jnp.arange(H, dtype=jnp.float32)))
    idx = jnp.arange(C, dtype=jnp.float32)
    rel = idx[:, None] - idx[None, :]
    decay_mask = jnp.where(rel[None] >= 0,
                           jnp.exp(jnp.maximum(rel, 0.0)[None] * log_gamma[:, None, None]), 0.0)
    q_decay = jnp.exp((idx + 1.0)[:, None] * log_gamma[None, :])[None, :, :, None]
    k_decay = jnp.exp((C - 1.0 - idx)[:, None] * log_gamma[None, :])[None, :, :, None]
    chunk_decay = jnp.exp(C * log_gamma)[None, :, None, None]
    n_chunks = T // C
    chunks = lambda z: jnp.moveaxis(z.reshape(B, n_chunks, C, H, z.shape[-1]), 1, 0)

    def step(R, inp):
        qc, kc, vc = inp
        s = jnp.einsum('bihd,bjhd->bhij', qc, kc) * decay_mask
        inner = jnp.einsum('bhij,bjhe->bihe', s, vc)
        cross = jnp.einsum('bihd,bhde->bihe', qc * q_decay, R)
        R = R * chunk_decay + jnp.einsum('bjhd,bjhe->bhde', kc * k_decay, vc)
        return R, inner + cross

    R0 = jnp.zeros((B, H, dk, dv), jnp.float32)
    _, o = lax.scan(step, R0, (chunks(q), chunks(k), chunks(v)))
    o = jnp.moveaxis(o, 0, 1).reshape(B, T, H, dv)
    o = group_norm_heads(o, RET_GN_EPS).reshape(B, T, RET_V_WIDTH)
    return jax.nn.silu(g) * o


def conv_ffn(h, w_up, conv_w, conv_b, w_down):
    u = h @ w_up
    T = u.shape[1]
    up = jnp.pad(u, ((0, 0), (CONV_WIDTH - 1, 0), (0, 0)))
    u = sum(up[:, i:i + T] * conv_w[i] for i in range(CONV_WIDTH)) + conv_b
    gate, val = jnp.split(u, 2, axis=-1)
    return (jax.nn.gelu(gate, approximate=True) * val) @ w_down


def setup_inputs(seed: int = 0) -> dict:
    key = jax.random.key(seed)
    ks = jax.random.split(key, 26)
    f32 = jnp.float32
    L = DEPTH
    nrm = lambda k, shape, s: s * jax.random.normal(k, shape, f32)
    x = jax.random.normal(ks[0], (BATCH, SEQ, D_MODEL), f32)
    offsets = jax.random.randint(ks[1], (BATCH, 1), 0, 4096, dtype=jnp.int32)
    positions = offsets + jnp.arange(SEQ, dtype=jnp.int32)[None, :]
    return {
        "x": x,
        "positions": positions,
        "norm_mix_pre": 1.0 + nrm(ks[2], (L, D_MODEL), 0.1),
        "norm_mix_post": 1.0 + nrm(ks[3], (L, D_MODEL), 0.1),
        "norm_ffn_pre": 1.0 + nrm(ks[4], (L, D_MODEL), 0.1),
        "norm_ffn_post": 1.0 + nrm(ks[5], (L, D_MODEL), 0.1),
        "w_in": nrm(ks[6], (L, D_MODEL, IN_COLS), D_MODEL ** -0.5),
        "rw_mu": jax.random.uniform(ks[7], (L, RW_COLS), f32, 0.0, 1.0),
        "rw_w0": jax.random.uniform(ks[8], (L, RW_WIDTH), f32, -6.0, 1.0),
        "rw_w2": nrm(ks[9], (L, RW_DECAY_RANK, RW_WIDTH), 0.1 * RW_DECAY_RANK ** -0.5),
        "rw_a0": nrm(ks[10], (L, RW_WIDTH), 0.1),
        "rw_a2": nrm(ks[11], (L, RW_ICLR_RANK, RW_WIDTH), 0.5 * RW_ICLR_RANK ** -0.5),
        "rw_g2": nrm(ks[12], (L, RW_GATE_RANK, RW_WIDTH), RW_GATE_RANK ** -0.5),
        "rw_k_k": 0.85 + nrm(ks[13], (L, RW_WIDTH), 0.1),
        "rw_k_a": 1.0 + nrm(ks[14], (L, RW_WIDTH), 0.1),
        "rw_r_k": nrm(ks[15], (L, RW_WIDTH), 0.1),
        "rw_lnx_w": 1.0 + nrm(ks[16], (L, RW_WIDTH), 0.1),
        "rw_lnx_b": nrm(ks[17], (L, RW_WIDTH), 0.01),
        "w_branch_rw": nrm(ks[18], (L, RW_WIDTH, D_MODEL), RW_WIDTH ** -0.5),
        "w_branch_ret": nrm(ks[19], (L, RET_V_WIDTH, D_MODEL), RET_V_WIDTH ** -0.5),
        "w_out": nrm(ks[20], (L, D_MODEL, D_MODEL), D_MODEL ** -0.5),
        "ffn_w_up": nrm(ks[21], (L, D_MODEL, 2 * D_FF), D_MODEL ** -0.5),
        "ffn_conv_w": nrm(ks[22], (L, CONV_WIDTH, 2 * D_FF), CONV_WIDTH ** -0.5),
        "ffn_conv_b": nrm(ks[23], (L, 2 * D_FF), 0.01),
        "ffn_w_down": nrm(ks[24], (L, D_FF, D_MODEL), D_FF ** -0.5),
    }


def reference(x, positions, norm_mix_pre, norm_mix_post, norm_ffn_pre, norm_ffn_post, w_in,
              rw_mu, rw_w0, rw_w2, rw_a0, rw_a2, rw_g2, rw_k_k, rw_k_a, rw_r_k, rw_lnx_w, rw_lnx_b,
              w_branch_rw, w_branch_ret, w_out, ffn_w_up, ffn_conv_w, ffn_conv_b, ffn_w_down):
    h = x
    for l in range(DEPTH):
        hn = rms_norm(h, norm_mix_pre[l])
        p = hn @ w_in[l]
        p_rw, p_ret, g_rw, g_ret = _split(p, IN_SIZES)
        y_rw = rwkv7_mixer(p_rw, rw_mu[l], rw_w0[l], rw_w2[l], rw_a0[l], rw_a2[l], rw_g2[l],
                           rw_k_k[l], rw_k_a[l], rw_r_k[l], rw_lnx_w[l], rw_lnx_b[l])
        y_ret = retention_mixer(p_ret, positions)
        merged = (jax.nn.sigmoid(g_rw.astype(jnp.float32)) * (y_rw @ w_branch_rw[l])
                  + jax.nn.sigmoid(g_ret.astype(jnp.float32)) * (y_ret @ w_branch_ret[l]))
        h = h + rms_norm(merged @ w_out[l], norm_mix_post[l])
        f = conv_ffn(rms_norm(h, norm_ffn_pre[l]), ffn_w_up[l], ffn_conv_w[l], ffn_conv_b[l], ffn_w_down[l])
        h = h + rms_norm(f, norm_ffn_post[l])
    return h.astype(x.dtype)
```

```python
import functools
import math

import numpy as np
import jax
import jax.numpy as jnp
from jax import lax
from jax.experimental import pallas as pl
from jax.experimental.pallas import tpu as pltpu

F32 = jnp.float32
BF16 = jnp.bfloat16

D_MODEL = 1024
RW_HEAD_DIM = 64
RW_WIDTH = 512
RW_HEADS = 8
RW_DECAY_RANK = 64
RW_ICLR_RANK = 64
RW_GATE_RANK = 128
RW_COLS = 3 * RW_WIDTH + RW_DECAY_RANK + RW_ICLR_RANK + RW_GATE_RANK
RW_GN_EPS = 64e-5
RW_CHUNK = 64
RET_QK_DIM = 64
RET_QK_WIDTH = 512
RET_HEADS = 8
RET_V_DIM = 128
RET_V_WIDTH = 1024
RET_COLS = 2 * RET_QK_WIDTH + 2 * RET_V_WIDTH
RET_CHUNK = 128
ROPE_BASE = 10000.0
RET_GN_EPS = 1e-5
D_FF = 2816
CONV_WIDTH = 3
RMS_EPS = 1e-6

LANES = 128
SUBLANES = 8
VMEM_LIMIT = 56 * 1024 * 1024
TOKEN_TILE = 512
SEQ_TILE = 512
FF_CHUNK = 256


def _dot(a, b):
    return jnp.dot(a.astype(BF16), b.astype(BF16), preferred_element_type=F32)


def _dot_nt(a, b):
    return lax.dot_general(a.astype(BF16), b.astype(BF16), (((1,), (1,)), ((), ())),
                           preferred_element_type=F32)


def _dot_tn(a, b):
    return lax.dot_general(a.astype(BF16), b.astype(BF16), (((0,), (0,)), ((), ())),
                           preferred_element_type=F32)


def _split2(x):
    hi = x.astype(BF16)
    lo = (x - hi.astype(F32)).astype(BF16)
    return hi, lo


def _dot_hilo(x, m):
    hi, lo = _split2(x)
    return (jnp.dot(hi, m, preferred_element_type=F32)
            + jnp.dot(lo, m, preferred_element_type=F32))


def _rms_norm(x, g):
    ms = jnp.mean(x * x, axis=-1, keepdims=True)
    return x * lax.rsqrt(ms + RMS_EPS) * g


def _softplus(z):
    return jnp.maximum(z, 0.0) + jnp.log1p(jnp.exp(-jnp.abs(z)))


def _const_spec(shape):
    nd = len(shape)
    return pl.BlockSpec(shape, lambda *_: (0,) * nd, pipeline_mode=pl.Buffered(1))


def _proj_kernel(x_ref, g_ref, wrw_ref, wret_ref, wg_ref, prw_ref, pret_ref, pg_ref):
    hn = _rms_norm(x_ref[...], g_ref[...]).astype(BF16)
    prw_ref[...] = jnp.dot(hn, wrw_ref[...], preferred_element_type=F32).astype(BF16)
    pret_ref[...] = jnp.dot(hn, wret_ref[...], preferred_element_type=F32).astype(BF16)
    pg_ref[...] = jnp.dot(hn, wg_ref[...], preferred_element_type=F32).astype(BF16)


def _input_projection(x2, g, w_rw, w_ret, w_gate, tm):
    n = x2.shape[0]
    row = lambda w: pl.BlockSpec((tm, w), lambda i: (i, 0))
    return pl.pallas_call(
        _proj_kernel,
        grid=(n // tm,),
        in_specs=[row(D_MODEL), _const_spec((1, D_MODEL)), _const_spec(w_rw.shape),
                  _const_spec(w_ret.shape), _const_spec(w_gate.shape)],
        out_specs=[row(RW_COLS), row(RET_COLS), row(2 * D_MODEL)],
        out_shape=[jax.ShapeDtypeStruct((n, RW_COLS), BF16),
                   jax.ShapeDtypeStruct((n, RET_COLS), BF16),
                   jax.ShapeDtypeStruct((n, 2 * D_MODEL), BF16)],
        compiler_params=pltpu.CompilerParams(dimension_semantics=("arbitrary",),
                                             vmem_limit_bytes=VMEM_LIMIT),
        name="input_projection",
    )(x2, g, w_rw, w_ret, w_gate)


def _rwkv_kernel(p_ref, mu_ref, w0_ref, wa_ref, a0_ref, g2_ref, kk_ref, ka_ref, rk_ref,
                 lnw_ref, lnb_ref, gsum_ref, tril_ref, o_ref,
                 r_s, k_s, v_s, a_s, b_s, lw_s, y_s, state_s, prev_s):
    tt = p_ref.shape[1]
    W = RW_WIDTH
    C = RW_CHUNK

    @pl.when(pl.program_id(1) == 0)
    def _():
        state_s[...] = jnp.zeros_like(state_s)
        prev_s[...] = jnp.zeros_like(prev_s)

    p = p_ref[0].astype(F32)
    row = lax.broadcasted_iota(jnp.int32, (tt, 1), 0)
    shifted = jnp.where(row == 0, prev_s[0:1, :], pltpu.roll(p, 1, 0))
    prev_s[0:1, :] = p[tt - 1:tt, :]
    p = p + (shifted - p) * mu_ref[...]

    r = p[:, 0:W]
    k = p[:, W:2 * W]
    v = p[:, 2 * W:3 * W]
    lane = lax.broadcasted_iota(jnp.int32, (1, LANES), 1)
    wa_in = p[:, 3 * W:3 * W + LANES]
    wa_in = jnp.where(lane < RW_DECAY_RANK, jnp.tanh(wa_in), wa_in)
    wa = _dot(wa_in, wa_ref[...])
    gd = p[:, 3 * W + LANES:3 * W + 2 * LANES]
    g = _dot(jax.nn.sigmoid(gd), g2_ref[...])

    w = -_softplus(-(w0_ref[...] + wa[:, 0:W])) - 0.5
    lw_s[...] = -jnp.exp(w)
    a_lr = jax.nn.sigmoid(a0_ref[...] + wa[:, W:2 * W])

    gsum = gsum_ref[...]
    kk = k * kk_ref[...]
    kk = kk / jnp.maximum(jnp.sqrt(_dot_hilo(kk * kk, gsum)), 1e-12)
    k = k * (1.0 + (a_lr - 1.0) * ka_ref[...])
    bonus = _dot_hilo(r * k * rk_ref[...], gsum) * v

    r_s[...] = r
    k_s[...] = k
    v_s[...] = v
    a_s[...] = -kk
    b_s[...] = kk * a_lr

    lane2 = lax.broadcasted_iota(jnp.int32, (C, LANES), 1)
    trow = lax.broadcasted_iota(jnp.int32, (C, LANES), 0)
    m0 = lane2 < RW_HEAD_DIM
    scol = jnp.where(m0, lane2, lane2 - RW_HEAD_DIM)
    strict = trow > scol
    incl = trow >= scol
    brow = lax.broadcasted_iota(jnp.int32, (LANES, LANES), 0) < RW_HEAD_DIM
    bcol = lax.broadcasted_iota(jnp.int32, (LANES, LANES), 1) < RW_HEAD_DIM
    blockmask = brow == bcol
    tril = tril_ref[...]

    def bd(x):
        z = jnp.zeros_like(x)
        return jnp.concatenate([jnp.where(m0, x, z), jnp.where(m0, z, x)], axis=0)

    def chunk_body(c, carry):
        rows = pl.ds(pl.multiple_of(c * C, C), C)
        lw = lw_s[rows, :]
        h1 = lw.astype(BF16)
        r1 = lw - h1.astype(F32)
        h2 = r1.astype(BF16)
        h3 = (r1 - h2.astype(F32)).astype(BF16)
        cum = (jnp.dot(tril, h1, preferred_element_type=F32)
               + jnp.dot(tril, h2, preferred_element_type=F32)
               + jnp.dot(tril, h3, preferred_element_type=F32))
        tot = cum[C - 1:C, :]
        w_in = jnp.exp(cum)
        w_ex = jnp.exp(cum - lw)
        w_inv = jnp.exp(-cum)
        w_rem = jnp.exp(tot - cum)
        w_tot = jnp.exp(tot)
        rc = r_s[rows, :]
        kc = k_s[rows, :]
        vc = v_s[rows, :].astype(BF16)
        ac = a_s[rows, :]
        bc = b_s[rows, :]
        rt = (rc * w_in).astype(BF16)
        at = (ac * w_ex).astype(BF16)
        bt = (bc * w_inv).astype(BF16)
        kt = (kc * w_inv).astype(BF16)
        bh = (bc * w_rem).astype(BF16)
        kh = (kc * w_rem).astype(BF16)
        for j in range(RW_HEADS // 2):
            sl = slice(LANES * j, LANES * (j + 1))
            lhs = jnp.concatenate([at[:, sl], rt[:, sl]], axis=0)
            gram = _dot_nt(lhs, jnp.concatenate([bd(bt[:, sl]), bd(kt[:, sl])], axis=0))
            s_old = state_s[j]
            proj = _dot_nt(lhs, s_old)
            zero = jnp.zeros((C, LANES), F32)
            a_ab = jnp.where(strict, gram[0:C, 0:LANES], zero)
            a_ak = jnp.where(strict, gram[0:C, LANES:2 * LANES], zero)
            a_rb = jnp.where(incl, gram[C:2 * C, 0:LANES], zero)
            a_rk = jnp.where(incl, gram[C:2 * C, LANES:2 * LANES], zero)
            vj = vc[:, sl]
            bdv = bd(vj)
            u = proj[0:C] + _dot(a_ak, bdv)
            nmat = a_ab.astype(BF16)
            n_levels = int(math.log2(C))
            for lvl in range(n_levels):
                ub = u.astype(BF16)
                if lvl < n_levels - 1:
                    comb = _dot(nmat, jnp.concatenate([bd(ub), bd(nmat)], axis=1))
                    u = u + comb[:, 0:LANES]
                    nmat = comb[:, LANES:2 * LANES].astype(BF16)
                else:
                    u = u + _dot(nmat, bd(ub))
            ub = u.astype(BF16)
            y = proj[C:2 * C] + _dot(jnp.concatenate([a_rb, a_rk], axis=1),
                                     jnp.concatenate([bd(ub), bdv], axis=0))
            y_s[rows, sl] = y
            upd = _dot_tn(jnp.concatenate([ub, vj], axis=0),
                          jnp.concatenate([bh[:, sl], kh[:, sl]], axis=0))
            state_s[j] = s_old * w_tot[:, sl] + jnp.where(blockmask, upd, jnp.zeros_like(upd))
        return carry

    lax.fori_loop(0, tt // C, chunk_body, 0)

    y = y_s[...]
    inv_n = 1.0 / RW_HEAD_DIM
    mean = _dot_hilo(y, gsum) * inv_n
    yc = y - mean
    var = _dot_hilo(yc * yc, gsum) * inv_n
    yn = yc * lax.rsqrt(var + RW_GN_EPS) * lnw_ref[...] + lnb_ref[...]
    o_ref[0] = ((yn + bonus) * g).astype(o_ref.dtype)


def _rwkv_mixer(p_rw, mu, w0, wa, a0, g2, k_k, k_a, r_k, lnw, lnb, tt):
    b, t, _ = p_rw.shape
    gsum = jnp.asarray(np.kron(np.eye(RW_HEADS), np.ones((RW_HEAD_DIM, RW_HEAD_DIM))), BF16)
    tril = jnp.asarray(np.tril(np.ones((RW_CHUNK, RW_CHUNK))), BF16)
    vec = lambda: _const_spec((1, RW_WIDTH))
    seq_scratch = lambda: pltpu.VMEM((tt, RW_WIDTH), F32)
    return pl.pallas_call(
        _rwkv_kernel,
        grid=(b, t // tt),
        in_specs=[pl.BlockSpec((1, tt, RW_COLS), lambda i, j: (i, j, 0)),
                  _const_spec((1, RW_COLS)), vec(), _const_spec(wa.shape), vec(),
                  _const_spec(g2.shape), vec(), vec(), vec(), vec(), vec(),
                  _const_spec(gsum.shape), _const_spec(tril.shape)],
        out_specs=pl.BlockSpec((1, tt, RW_WIDTH), lambda i, j: (i, j, 0)),
        out_shape=jax.ShapeDtypeStruct((b, t, RW_WIDTH), BF16),
        scratch_shapes=[seq_scratch() for _ in range(7)]
        + [pltpu.VMEM((RW_HEADS // 2, LANES, LANES), F32),
           pltpu.VMEM((SUBLANES, RW_COLS), F32)],
        compiler_params=pltpu.CompilerParams(dimension_semantics=("arbitrary", "arbitrary"),
                                             vmem_limit_bytes=VMEM_LIMIT),
        name="rwkv7_mixer",
    )(p_rw, mu, w0, wa, a0, g2, k_k, k_a, r_k, lnw, lnb, gsum, tril)


def _retention_kernel(p_ref, pos_ref, ones_ref, o_ref, q_s, k_s, o_s, dmask_s, state_s):
    tt = p_ref.shape[1]
    C = RET_CHUNK
    QW = RET_QK_WIDTH
    half = RET_QK_DIM // 2
    n_pairs = RET_HEADS // 2

    @pl.when(pl.program_id(1) == 0)
    def _():
        state_s[...] = jnp.zeros_like(state_s)

    lane = lax.broadcasted_iota(jnp.int32, (1, LANES), 1)
    freq = (lane & (half - 1)).astype(F32)
    inv = jnp.exp(freq * (-math.log(ROPE_BASE) / half))
    ang = pos_ref[0].astype(F32) * inv
    first = (lane & (RET_QK_DIM - 1)) < half
    cos = jnp.cos(ang)
    sin = jnp.where(first, -jnp.sin(ang), jnp.sin(ang))

    def rope(z):
        swapped = jnp.where(first, pltpu.roll(z, LANES - half, 1), pltpu.roll(z, half, 1))
        return z * cos + swapped * sin

    for j in range(n_pairs):
        sl = slice(LANES * j, LANES * (j + 1))
        q_s[:, sl] = rope(p_ref[0, :, sl].astype(F32))
        k_s[:, sl] = rope(p_ref[0, :, QW + LANES * j:QW + LANES * (j + 1)].astype(F32)) * (RET_QK_DIM ** -0.5)

    lane2 = lax.broadcasted_iota(jnp.int32, (C, 2 * LANES), 1)
    trow2 = lax.broadcasted_iota(jnp.int32, (C, 2 * LANES), 0)
    lane1 = lax.broadcasted_iota(jnp.int32, (C, LANES), 1)
    trow1 = lax.broadcasted_iota(jnp.int32, (C, LANES), 0).astype(F32)
    m0 = lane1 < RET_QK_DIM
    q_decay, k_decay, chunk_decay = [], [], []
    for j in range(n_pairs):
        lg = lambda hsel: jnp.log1p(-jnp.exp2(-5.0 - (2 * j + hsel).astype(F32)))
        lg2 = lg(lane2 >> 7)
        rel = (trow2 - (lane2 & (LANES - 1))).astype(F32)
        dmask_s[j] = jnp.where(rel >= 0, jnp.exp(jnp.maximum(rel, 0.0) * lg2), 0.0)
        lg1 = lg(lane1 >> 6)
        q_decay.append(jnp.exp((trow1 + 1.0) * lg1))
        k_decay.append(jnp.exp((C - 1.0 - trow1) * lg1))
        srow = lax.broadcasted_iota(jnp.int32, (LANES, LANES), 0) >> 6
        chunk_decay.append(jnp.exp(C * lg(srow)))

    def bd(x):
        z = jnp.zeros_like(x)
        return jnp.concatenate([jnp.where(m0, x, z), jnp.where(m0, z, x)], axis=0)

    def chunk_body(c, carry):
        rows = pl.ds(pl.multiple_of(c * C, C), C)
        for j in range(n_pairs):
            sl = slice(LANES * j, LANES * (j + 1))
            qj = q_s[rows, sl]
            kj = k_s[rows, sl]
            scores = _dot_nt(qj, bd(kj.astype(BF16))) * dmask_s[j]
            qd = (qj * q_decay[j]).astype(BF16)
            zq = jnp.zeros_like(qd)
            r_old = state_s[j]
            rb = r_old.astype(BF16)
            vpair = p_ref[0, rows, 2 * QW + 2 * LANES * j:2 * QW + 2 * LANES * (j + 1)]
            for hh in range(2):
                h = 2 * j + hh
                qm = jnp.where(m0, qd, zq) if hh == 0 else jnp.where(m0, zq, qd)
                lhs = jnp.concatenate([scores[:, LANES * hh:LANES * (hh + 1)].astype(BF16), qm], axis=1)
                rhs = jnp.concatenate([vpair[:, LANES * hh:LANES * (hh + 1)], rb], axis=0)
                o_s[rows, LANES * h:LANES * (h + 1)] = jnp.dot(lhs, rhs, preferred_element_type=F32)
            kd = (kj * k_decay[j]).astype(BF16)
            x = _dot_tn(kd, vpair)
            hrow = lax.broadcasted_iota(jnp.int32, (LANES, LANES), 0) < RET_QK_DIM
            state_s[j] = r_old * chunk_decay[j] + jnp.where(hrow, x[:, 0:LANES], x[:, LANES:2 * LANES])
        return carry

    lax.fori_loop(0, tt // C, chunk_body, 0)

    ones = ones_ref[...]
    inv_n = 1.0 / RET_V_DIM
    for h in range(RET_HEADS):
        sl = slice(LANES * h, LANES * (h + 1))
        o = o_s[:, sl]
        mean = _dot_hilo(o, ones) * inv_n
        oc = o - mean
        var = _dot_hilo(oc * oc, ones) * inv_n
        gate = p_ref[0, :, 2 * QW + RET_V_WIDTH + LANES * h:2 * QW + RET_V_WIDTH + LANES * (h + 1)].astype(F32)
        o_ref[0, :, sl] = (gate * jax.nn.sigmoid(gate) * oc * lax.rsqrt(var + RET_GN_EPS)).astype(o_ref.dtype)


def _retention_mixer(p_ret, pos3, tt):
    b, t, _ = p_ret.shape
    ones = jnp.ones((LANES, LANES), BF16)
    n_pairs = RET_HEADS // 2
    return pl.pallas_call(
        _retention_kernel,
        grid=(b, t // tt),
        in_specs=[pl.BlockSpec((1, tt, RET_COLS), lambda i, j: (i, j, 0)),
                  pl.BlockSpec((1, tt, 1), lambda i, j: (i, j, 0)),
                  _const_spec(ones.shape)],
        out_specs=pl.BlockSpec((1, tt, RET_V_WIDTH), lambda i, j: (i, j, 0)),
        out_shape=jax.ShapeDtypeStruct((b, t, RET_V_WIDTH), BF16),
        scratch_shapes=[pltpu.VMEM((tt, RET_QK_WIDTH), F32), pltpu.VMEM((tt, RET_QK_WIDTH), F32),
                        pltpu.VMEM((tt, RET_V_WIDTH), F32),
                        pltpu.VMEM((n_pairs, RET_CHUNK, 2 * LANES), F32),
                        pltpu.VMEM((n_pairs, LANES, LANES), F32)],
        compiler_params=pltpu.CompilerParams(dimension_semantics=("arbitrary", "arbitrary"),
                                             vmem_limit_bytes=VMEM_LIMIT),
        name="retention_mixer",
    )(p_ret, pos3, ones)


def _merge_kernel(x_ref, yrw_ref, yret_ref, pg_ref, wbrw_ref, wbret_ref, wout_ref,
                  npost_ref, nffn_ref, h_ref, hn_ref):
    g_rw = jax.nn.sigmoid(pg_ref[:, 0:D_MODEL].astype(F32))
    g_ret = jax.nn.sigmoid(pg_ref[:, D_MODEL:2 * D_MODEL].astype(F32))
    merged = (g_rw * jnp.dot(yrw_ref[...], wbrw_ref[...], preferred_element_type=F32)
              + g_ret * jnp.dot(yret_ref[...], wbret_ref[...], preferred_element_type=F32))
    mixed = _dot(merged, wout_ref[...])
    h = x_ref[...] + _rms_norm(mixed, npost_ref[...])
    h_ref[...] = h
    hn_ref[...] = _rms_norm(h, nffn_ref[...]).astype(BF16)


def _merge(x2, y_rw, y_ret, p_gate, wb_rw, wb_ret, w_out, n_post, n_ffn, tm):
    n = x2.shape[0]
    row = lambda w: pl.BlockSpec((tm, w), lambda i: (i, 0))
    return pl.pallas_call(
        _merge_kernel,
        grid=(n // tm,),
        in_specs=[row(D_MODEL), row(RW_WIDTH), row(RET_V_WIDTH), row(2 * D_MODEL),
                  _const_spec(wb_rw.shape), _const_spec(wb_ret.shape), _const_spec(w_out.shape),
                  _const_spec((1, D_MODEL)), _const_spec((1, D_MODEL))],
        out_specs=[row(D_MODEL), row(D_MODEL)],
        out_shape=[jax.ShapeDtypeStruct((n, D_MODEL), F32),
                   jax.ShapeDtypeStruct((n, D_MODEL), BF16)],
        compiler_params=pltpu.CompilerParams(dimension_semantics=("arbitrary",),
                                             vmem_limit_bytes=VMEM_LIMIT),
        name="merge_out_projection",
    )(x2, y_rw, y_ret, p_gate, wb_rw, wb_ret, w_out, n_post, n_ffn)


def _gelu_tanh(x):
    c = math.sqrt(2.0 / math.pi)
    return 0.5 * x * (1.0 + jnp.tanh(c * (x + 0.044715 * (x * x * x))))


def _ffn_kernel(hn_ref, h_ref, wup_ref, cw_ref, cb_ref, wdn_ref, npost_ref, o_ref, carry_s, acc_s):
    tt = hn_ref.shape[1]
    fc = FF_CHUNK

    @pl.when(pl.program_id(1) == 0)
    def _():
        carry_s[...] = jnp.zeros_like(carry_s)

    hn = hn_ref[0]
    row = lax.broadcasted_iota(jnp.int32, (tt, 1), 0)

    def conv(u, col):
        prev = carry_s[:, col:col + fc]
        s1 = jnp.where(row == 0, prev[SUBLANES - 1:SUBLANES, :], pltpu.roll(u, 1, 0))
        s2 = pltpu.roll(u, 2, 0)
        s2 = jnp.where(row == 0, prev[SUBLANES - 2:SUBLANES - 1, :], s2)
        s2 = jnp.where(row == 1, prev[SUBLANES - 1:SUBLANES, :], s2)
        carry_s[:, col:col + fc] = u[tt - SUBLANES:tt, :]
        cw = cw_ref[:, col:col + fc]
        return u * cw[2:3, :] + s1 * cw[1:2, :] + s2 * cw[0:1, :] + cb_ref[:, col:col + fc]

    for j in range(D_FF // fc):
        cg = j * fc
        cv = D_FF + j * fc
        gate = conv(jnp.dot(hn, wup_ref[:, cg:cg + fc], preferred_element_type=F32), cg)
        val = conv(jnp.dot(hn, wup_ref[:, cv:cv + fc], preferred_element_type=F32), cv)
        act = (_gelu_tanh(gate) * val).astype(BF16)
        part = jnp.dot(act, wdn_ref[cg:cg + fc, :], preferred_element_type=F32)
        if j == 0:
            acc_s[...] = part
        else:
            acc_s[...] += part

    o_ref[0] = h_ref[0] + _rms_norm(acc_s[...], npost_ref[...])


def _conv_ffn(hn3, h3, w_up, conv_w, conv_b, w_down, n_post, tt):
    b, t, _ = hn3.shape
    tile = lambda: pl.BlockSpec((1, tt, D_MODEL), lambda i, j: (i, j, 0))
    return pl.pallas_call(
        _ffn_kernel,
        grid=(b, t // tt),
        in_specs=[tile(), tile(), _const_spec(w_up.shape), _const_spec(conv_w.shape),
                  _const_spec(conv_b.shape), _const_spec(w_down.shape), _const_spec((1, D_MODEL))],
        out_specs=tile(),
        out_shape=jax.ShapeDtypeStruct((b, t, D_MODEL), F32),
        scratch_shapes=[pltpu.VMEM((SUBLANES, 2 * D_FF), F32), pltpu.VMEM((tt, D_MODEL), F32)],
        compiler_params=pltpu.CompilerParams(dimension_semantics=("arbitrary", "arbitrary"),
                                             vmem_limit_bytes=VMEM_LIMIT),
        name="conv_ffn",
    )(hn3, h3, w_up, conv_w, conv_b, w_down, n_post)


def kernel(x, positions, norm_mix_pre, norm_mix_post, norm_ffn_pre, norm_ffn_post, w_in, rw_mu, rw_w0, rw_w2, rw_a0, rw_a2, rw_g2, rw_k_k, rw_k_a, rw_r_k, rw_lnx_w, rw_lnx_b, w_branch_rw, w_branch_ret, w_out, ffn_w_up, ffn_conv_w, ffn_conv_b, ffn_w_down):
    b, t, d = x.shape
    assert d == D_MODEL and norm_mix_pre.shape[0] == 1
    n = b * t
    tm = min(TOKEN_TILE, n)
    tt = min(SEQ_TILE, t)
    assert n % tm == 0 and t % tt == 0 and tt % RET_CHUNK == 0
    vec = lambda a: a[0].reshape(1, -1).astype(F32)

    w_in_b = w_in[0].astype(BF16)
    w_rw = w_in_b[:, 0:RW_COLS]
    w_ret = w_in_b[:, RW_COLS:RW_COLS + RET_COLS]
    w_gate = w_in_b[:, RW_COLS + RET_COLS:]
    zeros = jnp.zeros((RW_DECAY_RANK, RW_WIDTH), F32)
    wa = jnp.concatenate([jnp.concatenate([rw_w2[0], zeros], axis=1),
                          jnp.concatenate([zeros, rw_a2[0]], axis=1)], axis=0).astype(BF16)

    x2 = x.reshape(n, d)
    p_rw, p_ret, p_gate = _input_projection(x2, vec(norm_mix_pre), w_rw, w_ret, w_gate, tm)

    y_rw = _rwkv_mixer(p_rw.reshape(b, t, RW_COLS), vec(rw_mu), vec(rw_w0), wa, vec(rw_a0),
                       rw_g2[0].astype(BF16), vec(rw_k_k), vec(rw_k_a), vec(rw_r_k),
                       vec(rw_lnx_w), vec(rw_lnx_b), tt)
    y_ret = _retention_mixer(p_ret.reshape(b, t, RET_COLS), positions.reshape(b, t, 1), tt)

    h, hn = _merge(x2, y_rw.reshape(n, RW_WIDTH), y_ret.reshape(n, RET_V_WIDTH), p_gate,
                   w_branch_rw[0].astype(BF16), w_branch_ret[0].astype(BF16), w_out[0].astype(BF16),
                   vec(norm_mix_post), vec(norm_ffn_pre), tm)

    out = _conv_ffn(hn.reshape(b, t, d), h.reshape(b, t, d), ffn_w_up[0].astype(BF16),
                    ffn_conv_w[0].astype(F32), vec(ffn_conv_b), ffn_w_down[0].astype(BF16),
                    vec(norm_ffn_post), tt)
    return out.astype(x.dtype)
```

```python
import functools
import math

import numpy as np
import jax
import jax.numpy as jnp
from jax import lax
from jax.experimental import pallas as pl
from jax.experimental.pallas import tpu as pltpu

F32 = jnp.float32
BF16 = jnp.bfloat16

D_MODEL = 1024
RW_HEAD_DIM = 64
RW_WIDTH = 512
RW_HEADS = 8
RW_DECAY_RANK = 64
RW_ICLR_RANK = 64
RW_GATE_RANK = 128
RW_COLS = 3 * RW_WIDTH + RW_DECAY_RANK + RW_ICLR_RANK + RW_GATE_RANK
RW_GN_EPS = 64e-5
RW_CHUNK = 64
RW_GROUP = 2
RET_QK_DIM = 64
RET_QK_WIDTH = 512
RET_HEADS = 8
RET_V_DIM = 128
RET_V_WIDTH = 1024
RET_COLS = 2 * RET_QK_WIDTH + 2 * RET_V_WIDTH
RET_CHUNK = 128
ROPE_BASE = 10000.0
RET_GN_EPS = 1e-5
D_FF = 2816
CONV_WIDTH = 3
RMS_EPS = 1e-6

LANES = 128
SUBLANES = 8
VMEM_LIMIT = 56 * 1024 * 1024
TOKEN_TILE = 512
SEQ_TILE = 512
FF_CHUNK = 256
FF_DOWN_GROUP = 11


def _dot(a, b):
    return jnp.dot(a.astype(BF16), b.astype(BF16), preferred_element_type=F32)


def _dot_nt(a, b):
    return lax.dot_general(a.astype(BF16), b.astype(BF16), (((1,), (1,)), ((), ())),
                           preferred_element_type=F32)


def _dot_tn(a, b):
    return lax.dot_general(a.astype(BF16), b.astype(BF16), (((0,), (0,)), ((), ())),
                           preferred_element_type=F32)


def _group_sum(x, m):
    return jnp.dot(x.astype(BF16), m, preferred_element_type=F32)


def _rms_norm(x, g):
    ms = jnp.mean(x * x, axis=-1, keepdims=True)
    return x * lax.rsqrt(ms + RMS_EPS) * g


def _const_spec(shape):
    nd = len(shape)
    return pl.BlockSpec(shape, lambda *_: (0,) * nd, pipeline_mode=pl.Buffered(1))


def _proj_kernel(x_ref, g_ref, wrw_ref, wret_ref, wg_ref, prw_ref, pret_ref, pg_ref):
    hn = _rms_norm(x_ref[...], g_ref[...]).astype(BF16)
    prw_ref[...] = jnp.dot(hn, wrw_ref[...], preferred_element_type=F32).astype(BF16)
    pret_ref[...] = jnp.dot(hn, wret_ref[...], preferred_element_type=F32).astype(BF16)
    pg_ref[...] = jnp.dot(hn, wg_ref[...], preferred_element_type=F32).astype(BF16)


def _input_projection(x2, g, w_rw, w_ret, w_gate, tm):
    n = x2.shape[0]
    row = lambda w: pl.BlockSpec((tm, w), lambda i: (i, 0))
    return pl.pallas_call(
        _proj_kernel,
        grid=(n // tm,),
        in_specs=[row(D_MODEL), _const_spec((1, D_MODEL)), _const_spec(w_rw.shape),
                  _const_spec(w_ret.shape), _const_spec(w_gate.shape)],
        out_specs=[row(RW_COLS), row(RET_COLS), row(2 * D_MODEL)],
        out_shape=[jax.ShapeDtypeStruct((n, RW_COLS), BF16),
                   jax.ShapeDtypeStruct((n, RET_COLS), BF16),
                   jax.ShapeDtypeStruct((n, 2 * D_MODEL), BF16)],
        compiler_params=pltpu.CompilerParams(dimension_semantics=("arbitrary",),
                                             vmem_limit_bytes=VMEM_LIMIT),
        name="input_projection",
    )(x2, g, w_rw, w_ret, w_gate)


def _rwkv_kernel(p_ref, mu_ref, w0_ref, wa_ref, a0_ref, g2_ref, kk_ref, ka_ref, rk_ref,
                 lnw_ref, lnb_ref, gsum_ref, tril_ref, o_ref,
                 r_s, k_s, v_s, a_s, b_s, lw_s, y_s, rp_s, yv_s, x_s, e_s, wtot_s, state_s, slab_s):
    tt = p_ref.shape[1]
    W = RW_WIDTH
    C = RW_CHUNK

    @pl.when(pl.program_id(1) == 0)
    def _():
        state_s[...] = jnp.zeros_like(state_s)

    first_tile = pl.program_id(1) == 0
    blocks = []
    for blk in range(RW_COLS // LANES):
        sl = slice(blk * LANES, (blk + 1) * LANES)
        slab = slab_s.at[blk]
        slab[0:SUBLANES, :] = jnp.where(first_tile, 0.0, slab[tt:tt + SUBLANES, :])
        pb = p_ref[0, :, sl].astype(F32)
        slab[SUBLANES:SUBLANES + tt, :] = pb
        blocks.append(pb + (slab[SUBLANES - 1:SUBLANES - 1 + tt, :] - pb) * mu_ref[:, sl])
    p = jnp.concatenate(blocks, axis=1)

    r = p[:, 0:W]
    k = p[:, W:2 * W]
    v = p[:, 2 * W:3 * W]
    lane = lax.broadcasted_iota(jnp.int32, (1, LANES), 1)
    wa_in = p[:, 3 * W:3 * W + LANES]
    wa_in = jnp.where(lane < RW_DECAY_RANK, jnp.tanh(wa_in), wa_in)
    wa = _dot(wa_in, wa_ref[...])
    gd = p[:, 3 * W + LANES:3 * W + 2 * LANES]
    g = _dot(jax.nn.sigmoid(gd), g2_ref[...])

    lw_s[...] = -math.exp(-0.5) * jax.nn.sigmoid(w0_ref[...] + wa[:, 0:W])
    a_lr = jax.nn.sigmoid(a0_ref[...] + wa[:, W:2 * W])

    gsum = gsum_ref[...]
    kk = k * kk_ref[...]
    kk = kk * lax.rsqrt(jnp.maximum(_group_sum(kk * kk, gsum), 1e-24))
    k = k * (1.0 + (a_lr - 1.0) * ka_ref[...])
    bonus = _group_sum(r * k * rk_ref[...], gsum) * v

    r_s[...] = r
    k_s[...] = k
    v_s[...] = v
    a_s[...] = -kk
    b_s[...] = kk * a_lr

    lane2 = lax.broadcasted_iota(jnp.int32, (C, LANES), 1)
    trow = lax.broadcasted_iota(jnp.int32, (C, LANES), 0)
    m0 = lane2 < RW_HEAD_DIM
    scol = jnp.where(m0, lane2, lane2 - RW_HEAD_DIM)
    strict = trow > scol
    incl = trow >= scol
    brow = lax.broadcasted_iota(jnp.int32, (LANES, LANES), 0) < RW_HEAD_DIM
    bcol = lax.broadcasted_iota(jnp.int32, (LANES, LANES), 1) < RW_HEAD_DIM
    blockmask = brow == bcol
    tril = tril_ref[...]

    def bd(x):
        z = jnp.zeros_like(x)
        return jnp.concatenate([jnp.where(m0, x, z), jnp.where(m0, z, x)], axis=0)

    n_pairs = RW_HEADS // 2
    grp = RW_GROUP
    gc = grp * C
    grow = lax.broadcasted_iota(jnp.int32, (gc, 1), 0)
    n_levels = int(math.log2(C))

    def prep_body(i, carry):
        rows = pl.ds(pl.multiple_of(i * gc, gc), gc)
        lw = lw_s[rows, :]
        h1 = lw.astype(BF16)
        r1 = lw - h1.astype(F32)
        h2 = r1.astype(BF16)
        h3 = (r1 - h2.astype(F32)).astype(BF16)
        cum = (jnp.dot(tril, h1, preferred_element_type=F32)
               + jnp.dot(tril, h2, preferred_element_type=F32)
               + jnp.dot(tril, h3, preferred_element_type=F32))
        tot = cum[C - 1:C, :]
        for g in range(1, grp):
            tot = jnp.where(grow >= g * C, cum[(g + 1) * C - 1:(g + 1) * C, :], tot)
        for g in range(grp):
            wtot_s[pl.ds(pl.multiple_of((i * grp + g) * SUBLANES, SUBLANES), SUBLANES), :] = jnp.broadcast_to(
                jnp.exp(cum[(g + 1) * C - 1:(g + 1) * C, :]), (SUBLANES, W))
        w_inv = jnp.exp(-cum)
        w_rem = jnp.exp(tot - cum)
        rc = r_s[rows, :]
        kc = k_s[rows, :]
        bc = b_s[rows, :]
        vc = v_s[rows, :].astype(BF16)
        rt32 = rc * jnp.exp(cum)
        at32 = a_s[rows, :] * jnp.exp(cum - lw)
        rt = rt32.astype(BF16)
        at = at32.astype(BF16)
        bt = (bc * w_inv).astype(BF16)
        kt = (kc * w_inv).astype(BF16)
        bh = (bc * w_rem).astype(BF16)
        kh = (kc * w_rem).astype(BF16)

        units = [(g, j) for g in range(grp) for j in range(n_pairs)]
        blk = lambda arr, g, j: arr[g * C:(g + 1) * C, LANES * j:LANES * (j + 1)]
        zero = jnp.zeros((C, LANES), F32)
        a_rbk, z, nmat, bdv = {}, {}, {}, {}
        for u in units:
            lhs = jnp.concatenate([blk(at, *u), blk(rt, *u)], axis=0)
            gram = _dot_nt(lhs, jnp.concatenate([bd(blk(bt, *u)), bd(blk(kt, *u))], axis=0))
            nmat[u] = jnp.where(strict, gram[0:C, 0:LANES], zero).astype(BF16)
            z[u] = jnp.where(strict, gram[0:C, LANES:2 * LANES], zero)
            a_rbk[u] = jnp.concatenate([jnp.where(incl, gram[C:2 * C, 0:LANES], zero),
                                        jnp.where(incl, gram[C:2 * C, LANES:2 * LANES], zero)],
                                       axis=1).astype(BF16)
            bdv[u] = bd(blk(vc, *u))
        for u in units:
            z[u] = jnp.concatenate([blk(at32, *u), _dot(z[u], bdv[u])], axis=1)
        for lvl in range(n_levels):
            last = lvl == n_levels - 1
            for u in units:
                zb = z[u].astype(BF16)
                rhs = [bd(zb[:, 0:LANES]), bd(zb[:, LANES:2 * LANES])] + ([] if last else [bd(nmat[u])])
                comb = _dot(nmat[u], jnp.concatenate(rhs, axis=1))
                z[u] = z[u] + comb[:, 0:2 * LANES]
                if not last:
                    nmat[u] = comb[:, 2 * LANES:3 * LANES].astype(BF16)
        zblock = jnp.zeros((LANES, LANES), BF16)
        for u in units:
            g, j = u
            crow = pl.ds(pl.multiple_of(i * gc + g * C, C), C)
            sl = slice(LANES * j, LANES * (j + 1))
            idx = (i * grp + g) * n_pairs + j
            zb = z[u].astype(BF16)
            apb = zb[:, 0:LANES]
            uvb = zb[:, LANES:2 * LANES]
            rhs = jnp.concatenate([jnp.concatenate([bd(apb), bd(uvb)], axis=1),
                                   jnp.concatenate([zblock, bdv[u]], axis=1)], axis=0)
            out = jnp.dot(a_rbk[u], rhs, preferred_element_type=F32)
            rp_s[crow, sl] = blk(rt32, *u) + out[:, 0:LANES]
            yv_s[crow, sl] = out[:, LANES:2 * LANES]
            x = _dot_tn(apb, blk(bh, *u))
            x_s[idx] = jnp.where(blockmask, x, jnp.zeros_like(x)).astype(BF16)
            e = _dot_tn(jnp.concatenate([uvb, blk(vc, *u)], axis=0),
                        jnp.concatenate([blk(bh, *u), blk(kh, *u)], axis=0))
            e_s[idx] = jnp.where(blockmask, e, jnp.zeros_like(e))
        return carry

    lax.fori_loop(0, tt // gc, prep_body, 0)

    def scan_body(c, carry):
        rows = pl.ds(pl.multiple_of(c * C, C), C)
        w_tot = wtot_s[pl.ds(pl.multiple_of(c * SUBLANES, SUBLANES), SUBLANES), :][0:1, :]
        for j in range(n_pairs):
            sl = slice(LANES * j, LANES * (j + 1))
            s_old = state_s[j]
            sb = s_old.astype(BF16)
            y_s[rows, sl] = _dot_nt(rp_s[rows, sl], sb) + yv_s[rows, sl]
            state_s[j] = (s_old * w_tot[:, sl] + e_s[c * n_pairs + j]
                          + jnp.dot(sb, x_s[c * n_pairs + j], preferred_element_type=F32))
        return carry

    lax.fori_loop(0, tt // C, scan_body, 0)

    y = y_s[...]
    inv_n = 1.0 / RW_HEAD_DIM
    mean = _group_sum(y, gsum) * inv_n
    yc = y - mean
    var = _group_sum(yc * yc, gsum) * inv_n
    yn = yc * lax.rsqrt(var + RW_GN_EPS) * lnw_ref[...] + lnb_ref[...]
    o_ref[0] = ((yn + bonus) * g).astype(o_ref.dtype)


def _rwkv_mixer(p_rw, mu, w0, wa, a0, g2, k_k, k_a, r_k, lnw, lnb, tt):
    b, t, _ = p_rw.shape
    gsum = jnp.asarray(np.kron(np.eye(RW_HEADS), np.ones((RW_HEAD_DIM, RW_HEAD_DIM))), BF16)
    tril = jnp.asarray(np.kron(np.eye(RW_GROUP), np.tril(np.ones((RW_CHUNK, RW_CHUNK)))), BF16)
    n_chunks = tt // RW_CHUNK
    n_pairs = RW_HEADS // 2
    vec = lambda: _const_spec((1, RW_WIDTH))
    seq_scratch = lambda: pltpu.VMEM((tt, RW_WIDTH), F32)
    return pl.pallas_call(
        _rwkv_kernel,
        grid=(b, t // tt),
        in_specs=[pl.BlockSpec((1, tt, RW_COLS), lambda i, j: (i, j, 0)),
                  _const_spec((1, RW_COLS)), vec(), _const_spec(wa.shape), vec(),
                  _const_spec(g2.shape), vec(), vec(), vec(), vec(), vec(),
                  _const_spec(gsum.shape), _const_spec(tril.shape)],
        out_specs=pl.BlockSpec((1, tt, RW_WIDTH), lambda i, j: (i, j, 0)),
        out_shape=jax.ShapeDtypeStruct((b, t, RW_WIDTH), BF16),
        scratch_shapes=[seq_scratch() for _ in range(9)]
        + [pltpu.VMEM((n_chunks * n_pairs, LANES, LANES), BF16),
           pltpu.VMEM((n_chunks * n_pairs, LANES, LANES), F32),
           pltpu.VMEM((n_chunks * SUBLANES, RW_WIDTH), F32),
           pltpu.VMEM((n_pairs, LANES, LANES), F32),
           pltpu.VMEM((RW_COLS // LANES, SUBLANES + tt, LANES), F32)],
        compiler_params=pltpu.CompilerParams(dimension_semantics=("arbitrary", "arbitrary"),
                                             vmem_limit_bytes=VMEM_LIMIT),
        name="rwkv7_mixer",
    )(p_rw, mu, w0, wa, a0, g2, k_k, k_a, r_k, lnw, lnb, gsum, tril)


def _retention_kernel(p_ref, pos_ref, ones_ref, o_ref, q_s, k_s, o_s, dmask_s, state_s):
    tt = p_ref.shape[1]
    C = RET_CHUNK
    QW = RET_QK_WIDTH
    half = RET_QK_DIM // 2
    n_pairs = RET_HEADS // 2

    @pl.when(pl.program_id(1) == 0)
    def _():
        state_s[...] = jnp.zeros_like(state_s)

    lane = lax.broadcasted_iota(jnp.int32, (1, LANES), 1)
    freq = (lane & (half - 1)).astype(F32)
    inv = jnp.exp(freq * (-math.log(ROPE_BASE) / half))
    ang = pos_ref[0].astype(F32) * inv
    first = (lane & (RET_QK_DIM - 1)) < half
    cos = jnp.cos(ang)
    sin = jnp.where(first, -jnp.sin(ang), jnp.sin(ang))

    def rope(z):
        swapped = jnp.where(first, pltpu.roll(z, LANES - half, 1), pltpu.roll(z, half, 1))
        return z * cos + swapped * sin

    for j in range(n_pairs):
        sl = slice(LANES * j, LANES * (j + 1))
        q_s[:, sl] = rope(p_ref[0, :, sl].astype(F32))
        k_s[:, sl] = rope(p_ref[0, :, QW + LANES * j:QW + LANES * (j + 1)].astype(F32)) * (RET_QK_DIM ** -0.5)

    lane2 = lax.broadcasted_iota(jnp.int32, (C, 2 * LANES), 1)
    trow2 = lax.broadcasted_iota(jnp.int32, (C, 2 * LANES), 0)
    lane1 = lax.broadcasted_iota(jnp.int32, (C, LANES), 1)
    trow1 = lax.broadcasted_iota(jnp.int32, (C, LANES), 0).astype(F32)
    m0 = lane1 < RET_QK_DIM
    q_decay, k_decay, chunk_decay = [], [], []
    for j in range(n_pairs):
        lg = lambda hsel: jnp.log1p(-jnp.exp2(-5.0 - (2 * j + hsel).astype(F32)))
        lg2 = lg(lane2 >> 7)
        rel = (trow2 - (lane2 & (LANES - 1))).astype(F32)
        dmask_s[j] = jnp.where(rel >= 0, jnp.exp(jnp.maximum(rel, 0.0) * lg2), 0.0)
        lg1 = lg(lane1 >> 6)
        q_decay.append(jnp.exp((trow1 + 1.0) * lg1))
        k_decay.append(jnp.exp((C - 1.0 - trow1) * lg1))
        srow = lax.broadcasted_iota(jnp.int32, (LANES, LANES), 0) >> 6
        chunk_decay.append(jnp.exp(C * lg(srow)))

    def bd(x):
        z = jnp.zeros_like(x)
        return jnp.concatenate([jnp.where(m0, x, z), jnp.where(m0, z, x)], axis=0)

    def chunk_body(c, carry):
        rows = pl.ds(pl.multiple_of(c * C, C), C)
        for j in range(n_pairs):
            sl = slice(LANES * j, LANES * (j + 1))
            qj = q_s[rows, sl]
            kj = k_s[rows, sl]
            scores = _dot_nt(qj, bd(kj.astype(BF16))) * dmask_s[j]
            qd = (qj * q_decay[j]).astype(BF16)
            zq = jnp.zeros_like(qd)
            r_old = state_s[j]
            rb = r_old.astype(BF16)
            vpair = p_ref[0, rows, 2 * QW + 2 * LANES * j:2 * QW + 2 * LANES * (j + 1)]
            for hh in range(2):
                h = 2 * j + hh
                qm = jnp.where(m0, qd, zq) if hh == 0 else jnp.where(m0, zq, qd)
                lhs = jnp.concatenate([scores[:, LANES * hh:LANES * (hh + 1)].astype(BF16), qm], axis=1)
                rhs = jnp.concatenate([vpair[:, LANES * hh:LANES * (hh + 1)], rb], axis=0)
                o_s[rows, LANES * h:LANES * (h + 1)] = jnp.dot(lhs, rhs, preferred_element_type=F32)
            kd = (kj * k_decay[j]).astype(BF16)
            x = _dot_tn(kd, vpair)
            hrow = lax.broadcasted_iota(jnp.int32, (LANES, LANES), 0) < RET_QK_DIM
            state_s[j] = r_old * chunk_decay[j] + jnp.where(hrow, x[:, 0:LANES], x[:, LANES:2 * LANES])
        return carry

    lax.fori_loop(0, tt // C, chunk_body, 0)

    ones = ones_ref[...]
    inv_n = 1.0 / RET_V_DIM
    for h in range(RET_HEADS):
        sl = slice(LANES * h, LANES * (h + 1))
        o = o_s[:, sl]
        mean = _group_sum(o, ones) * inv_n
        oc = o - mean
        var = _group_sum(oc * oc, ones) * inv_n
        gate = p_ref[0, :, 2 * QW + RET_V_WIDTH + LANES * h:2 * QW + RET_V_WIDTH + LANES * (h + 1)].astype(F32)
        o_ref[0, :, sl] = (gate * jax.nn.sigmoid(gate) * oc * lax.rsqrt(var + RET_GN_EPS)).astype(o_ref.dtype)


def _retention_mixer(p_ret, pos3, tt):
    b, t, _ = p_ret.shape
    ones = jnp.ones((LANES, LANES), BF16)
    n_pairs = RET_HEADS // 2
    return pl.pallas_call(
        _retention_kernel,
        grid=(b, t // tt),
        in_specs=[pl.BlockSpec((1, tt, RET_COLS), lambda i, j: (i, j, 0)),
                  pl.BlockSpec((1, tt, 1), lambda i, j: (i, j, 0)),
                  _const_spec(ones.shape)],
        out_specs=pl.BlockSpec((1, tt, RET_V_WIDTH), lambda i, j: (i, j, 0)),
        out_shape=jax.ShapeDtypeStruct((b, t, RET_V_WIDTH), BF16),
        scratch_shapes=[pltpu.VMEM((tt, RET_QK_WIDTH), F32), pltpu.VMEM((tt, RET_QK_WIDTH), F32),
                        pltpu.VMEM((tt, RET_V_WIDTH), F32),
                        pltpu.VMEM((n_pairs, RET_CHUNK, 2 * LANES), F32),
                        pltpu.VMEM((n_pairs, LANES, LANES), F32)],
        compiler_params=pltpu.CompilerParams(dimension_semantics=("arbitrary", "arbitrary"),
                                             vmem_limit_bytes=VMEM_LIMIT),
        name="retention_mixer",
    )(p_ret, pos3, ones)


def _merge_kernel(x_ref, yrw_ref, yret_ref, pg_ref, wbrw_ref, wbret_ref, wout_ref,
                  npost_ref, nffn_ref, h_ref, hn_ref):
    g_rw = jax.nn.sigmoid(pg_ref[:, 0:D_MODEL].astype(F32))
    g_ret = jax.nn.sigmoid(pg_ref[:, D_MODEL:2 * D_MODEL].astype(F32))
    merged = (g_rw * jnp.dot(yrw_ref[...], wbrw_ref[...], preferred_element_type=F32)
              + g_ret * jnp.dot(yret_ref[...], wbret_ref[...], preferred_element_type=F32))
    mixed = _dot(merged, wout_ref[...])
    h = x_ref[...] + _rms_norm(mixed, npost_ref[...])
    h_ref[...] = h
    hn_ref[...] = _rms_norm(h, nffn_ref[...]).astype(BF16)


def _merge(x2, y_rw, y_ret, p_gate, wb_rw, wb_ret, w_out, n_post, n_ffn, tm):
    n = x2.shape[0]
    row = lambda w: pl.BlockSpec((tm, w), lambda i: (i, 0))
    return pl.pallas_call(
        _merge_kernel,
        grid=(n // tm,),
        in_specs=[row(D_MODEL), row(RW_WIDTH), row(RET_V_WIDTH), row(2 * D_MODEL),
                  _const_spec(wb_rw.shape), _const_spec(wb_ret.shape), _const_spec(w_out.shape),
                  _const_spec((1, D_MODEL)), _const_spec((1, D_MODEL))],
        out_specs=[row(D_MODEL), row(D_MODEL)],
        out_shape=[jax.ShapeDtypeStruct((n, D_MODEL), F32),
                   jax.ShapeDtypeStruct((n, D_MODEL), BF16)],
        compiler_params=pltpu.CompilerParams(dimension_semantics=("arbitrary",),
                                             vmem_limit_bytes=VMEM_LIMIT),
        name="merge_out_projection",
    )(x2, y_rw, y_ret, p_gate, wb_rw, wb_ret, w_out, n_post, n_ffn)


def _gelu_tanh(x):
    k0 = -2.0 * math.sqrt(2.0 / math.pi) * math.log2(math.e)
    k1 = k0 * 0.044715
    e = jnp.exp2(x * (k1 * (x * x) + k0))
    return x * (1.0 / (1.0 + e))


def _ffn_kernel(hn_ref, h_ref, wup_ref, cw_ref, cb_ref, wdn_ref, npost_ref, o_ref, slab_s, acc_s):
    tt = hn_ref.shape[1]
    fc = FF_CHUNK
    first_tile = pl.program_id(1) == 0

    hn = hn_ref[0]

    def conv(u, col):
        outs = []
        for blk in range(fc // LANES):
            c0 = col + blk * LANES
            slab = slab_s.at[c0 // LANES]
            prev = jnp.where(first_tile, 0.0, slab[tt:tt + SUBLANES, :])
            slab[0:SUBLANES, :] = prev
            ub = u[:, blk * LANES:(blk + 1) * LANES]
            slab[SUBLANES:SUBLANES + tt, :] = ub
            s1 = slab[SUBLANES - 1:SUBLANES - 1 + tt, :]
            s2 = slab[SUBLANES - 2:SUBLANES - 2 + tt, :]
            cw = cw_ref[:, c0:c0 + LANES]
            outs.append(ub * cw[2:3, :] + s1 * cw[1:2, :] + s2 * cw[0:1, :] + cb_ref[:, c0:c0 + LANES])
        return jnp.concatenate(outs, axis=1)

    def up(j):
        cg = j * fc
        cv = D_FF + j * fc
        return (jnp.dot(hn, wup_ref[:, cg:cg + fc], preferred_element_type=F32),
                jnp.dot(hn, wup_ref[:, cv:cv + fc], preferred_element_type=F32))

    n_chunks = D_FF // fc
    u_next = up(0)
    acts = []
    for j in range(n_chunks):
        cg = j * fc
        cv = D_FF + j * fc
        u_gate, u_val = u_next
        if j + 1 < n_chunks:
            u_next = up(j + 1)
        gate = conv(u_gate, cg)
        val = conv(u_val, cv)
        acts.append((_gelu_tanh(gate) * val).astype(BF16))
        if len(acts) == FF_DOWN_GROUP or j == n_chunks - 1:
            k0 = (j + 1 - len(acts)) * fc
            act = acts[0] if len(acts) == 1 else jnp.concatenate(acts, axis=1)
            part = jnp.dot(act, wdn_ref[k0:k0 + len(acts) * fc, :], preferred_element_type=F32)
            if k0 == 0:
                acc_s[...] = part
            else:
                acc_s[...] += part
            acts = []

    o_ref[0] = h_ref[0] + _rms_norm(acc_s[...], npost_ref[...])


def _conv_ffn(hn3, h3, w_up, conv_w, conv_b, w_down, n_post, tt):
    b, t, _ = hn3.shape
    tile = lambda: pl.BlockSpec((1, tt, D_MODEL), lambda i, j: (i, j, 0))
    return pl.pallas_call(
        _ffn_kernel,
        grid=(b, t // tt),
        in_specs=[tile(), tile(), _const_spec(w_up.shape), _const_spec(conv_w.shape),
                  _const_spec(conv_b.shape), _const_spec(w_down.shape), _const_spec((1, D_MODEL))],
        out_specs=tile(),
        out_shape=jax.ShapeDtypeStruct((b, t, D_MODEL), F32),
        scratch_shapes=[pltpu.VMEM((2 * D_FF // LANES, SUBLANES + tt, LANES), F32),
                        pltpu.VMEM((tt, D_MODEL), F32)],
        compiler_params=pltpu.CompilerParams(dimension_semantics=("arbitrary", "arbitrary"),
                                             vmem_limit_bytes=VMEM_LIMIT),
        name="conv_ffn",
    )(hn3, h3, w_up, conv_w, conv_b, w_down, n_post)


def kernel(x, positions, norm_mix_pre, norm_mix_post, norm_ffn_pre, norm_ffn_post, w_in, rw_mu, rw_w0, rw_w2, rw_a0, rw_a2, rw_g2, rw_k_k, rw_k_a, rw_r_k, rw_lnx_w, rw_lnx_b, w_branch_rw, w_branch_ret, w_out, ffn_w_up, ffn_conv_w, ffn_conv_b, ffn_w_down):
    b, t, d = x.shape
    assert d == D_MODEL and norm_mix_pre.shape[0] == 1
    n = b * t
    tm = min(TOKEN_TILE, n)
    tt = min(SEQ_TILE, t)
    assert n % tm == 0 and t % tt == 0 and tt % RET_CHUNK == 0
    vec = lambda a: a[0].reshape(1, -1).astype(F32)

    w_in_b = w_in[0].astype(BF16)
    w_rw = w_in_b[:, 0:RW_COLS]
    w_ret = w_in_b[:, RW_COLS:RW_COLS + RET_COLS]
    w_gate = w_in_b[:, RW_COLS + RET_COLS:]
    zeros = jnp.zeros((RW_DECAY_RANK, RW_WIDTH), F32)
    wa = jnp.concatenate([jnp.concatenate([rw_w2[0], zeros], axis=1),
                          jnp.concatenate([zeros, rw_a2[0]], axis=1)], axis=0).astype(BF16)

    x2 = x.reshape(n, d)
    p_rw, p_ret, p_gate = _input_projection(x2, vec(norm_mix_pre), w_rw, w_ret, w_gate, tm)

    y_rw = _rwkv_mixer(p_rw.reshape(b, t, RW_COLS), vec(rw_mu), vec(rw_w0), wa, vec(rw_a0),
                       rw_g2[0].astype(BF16), vec(rw_k_k), vec(rw_k_a), vec(rw_r_k),
                       vec(rw_lnx_w), vec(rw_lnx_b), tt)
    y_ret = _retention_mixer(p_ret.reshape(b, t, RET_COLS), positions.reshape(b, t, 1), tt)

    h, hn = _merge(x2, y_rw.reshape(n, RW_WIDTH), y_ret.reshape(n, RET_V_WIDTH), p_gate,
                   w_branch_rw[0].astype(BF16), w_branch_ret[0].astype(BF16), w_out[0].astype(BF16),
                   vec(norm_mix_post), vec(norm_ffn_pre), tm)

    out = _conv_ffn(hn.reshape(b, t, d), h.reshape(b, t, d), ffn_w_up[0].astype(BF16),
                    ffn_conv_w[0].astype(F32), vec(ffn_conv_b), ffn_w_down[0].astype(BF16),
                    vec(norm_ffn_post), tt)
    return out.astype(x.dtype)
```

```python
import functools
import math

import numpy as np
import jax
import jax.numpy as jnp
from jax import lax
from jax.experimental import pallas as pl
from jax.experimental.pallas import tpu as pltpu

F32 = jnp.float32
BF16 = jnp.bfloat16

D_MODEL = 1024
RW_HEAD_DIM = 64
RW_WIDTH = 512
RW_HEADS = 8
RW_DECAY_RANK = 64
RW_ICLR_RANK = 64
RW_GATE_RANK = 128
RW_COLS = 3 * RW_WIDTH + RW_DECAY_RANK + RW_ICLR_RANK + RW_GATE_RANK
RW_GN_EPS = 64e-5
RW_CHUNK = 64
RW_GROUP = 4
RET_QK_DIM = 64
RET_QK_WIDTH = 512
RET_HEADS = 8
RET_V_DIM = 128
RET_V_WIDTH = 1024
RET_COLS = 2 * RET_QK_WIDTH + 2 * RET_V_WIDTH
RET_CHUNK = 128
ROPE_BASE = 10000.0
RET_GN_EPS = 1e-5
D_FF = 2816
CONV_WIDTH = 3
RMS_EPS = 1e-6

LANES = 128
SUBLANES = 8
VMEM_LIMIT = 56 * 1024 * 1024
TOKEN_TILE = 512
SEQ_TILE = 512
FF_CHUNK = 256
FF_DOWN_GROUP = 11


def _dot(a, b):
    return jnp.dot(a.astype(BF16), b.astype(BF16), preferred_element_type=F32)


def _dot_nt(a, b):
    return lax.dot_general(a.astype(BF16), b.astype(BF16), (((1,), (1,)), ((), ())),
                           preferred_element_type=F32)


def _dot_tn(a, b):
    return lax.dot_general(a.astype(BF16), b.astype(BF16), (((0,), (0,)), ((), ())),
                           preferred_element_type=F32)


def _group_sum(x, m):
    return jnp.dot(x.astype(BF16), m, preferred_element_type=F32)


def _rms_norm(x, g):
    ms = jnp.mean(x * x, axis=-1, keepdims=True)
    return x * lax.rsqrt(ms + RMS_EPS) * g


def _const_spec(shape):
    nd = len(shape)
    return pl.BlockSpec(shape, lambda *_: (0,) * nd, pipeline_mode=pl.Buffered(1))


def _proj_kernel(x_ref, g_ref, wrw_ref, wret_ref, wg_ref, prw_ref, pret_ref, pg_ref):
    hn = _rms_norm(x_ref[...], g_ref[...]).astype(BF16)
    prw_ref[...] = jnp.dot(hn, wrw_ref[...], preferred_element_type=F32).astype(BF16)
    pret_ref[...] = jnp.dot(hn, wret_ref[...], preferred_element_type=F32).astype(BF16)
    pg_ref[...] = jnp.dot(hn, wg_ref[...], preferred_element_type=F32).astype(BF16)


def _input_projection(x2, g, w_rw, w_ret, w_gate, tm):
    n = x2.shape[0]
    row = lambda w: pl.BlockSpec((tm, w), lambda i: (i, 0))
    return pl.pallas_call(
        _proj_kernel,
        grid=(n // tm,),
        in_specs=[row(D_MODEL), _const_spec((1, D_MODEL)), _const_spec(w_rw.shape),
                  _const_spec(w_ret.shape), _const_spec(w_gate.shape)],
        out_specs=[row(RW_COLS), row(RET_COLS), row(2 * D_MODEL)],
        out_shape=[jax.ShapeDtypeStruct((n, RW_COLS), BF16),
                   jax.ShapeDtypeStruct((n, RET_COLS), BF16),
                   jax.ShapeDtypeStruct((n, 2 * D_MODEL), BF16)],
        compiler_params=pltpu.CompilerParams(dimension_semantics=("arbitrary",),
                                             vmem_limit_bytes=VMEM_LIMIT),
        name="input_projection",
    )(x2, g, w_rw, w_ret, w_gate)


def _rwkv_kernel(p_ref, mu_ref, w0_ref, wa_ref, a0_ref, g2_ref, kk_ref, ka_ref, rk_ref,
                 lnw_ref, lnb_ref, gsum_ref, tril_ref, o_ref,
                 r_s, k_s, v_s, a_s, b_s, lw_s, y_s, rp_s, yv_s, x_s, e_s, wtot_s, state_s, slab_s):
    tt = p_ref.shape[1]
    W = RW_WIDTH
    C = RW_CHUNK

    @pl.when(pl.program_id(1) == 0)
    def _():
        state_s[...] = jnp.zeros_like(state_s)

    first_tile = pl.program_id(1) == 0
    blocks = []
    for blk in range(RW_COLS // LANES):
        sl = slice(blk * LANES, (blk + 1) * LANES)
        slab = slab_s.at[blk]
        slab[0:SUBLANES, :] = jnp.where(first_tile, 0.0, slab[tt:tt + SUBLANES, :])
        pb = p_ref[0, :, sl].astype(F32)
        slab[SUBLANES:SUBLANES + tt, :] = pb
        blocks.append(pb + (slab[SUBLANES - 1:SUBLANES - 1 + tt, :] - pb) * mu_ref[:, sl])
    p = jnp.concatenate(blocks, axis=1)

    r = p[:, 0:W]
    k = p[:, W:2 * W]
    v = p[:, 2 * W:3 * W]
    lane = lax.broadcasted_iota(jnp.int32, (1, LANES), 1)
    wa_in = p[:, 3 * W:3 * W + LANES]
    wa_in = jnp.where(lane < RW_DECAY_RANK, jnp.tanh(wa_in), wa_in)
    wa = _dot(wa_in, wa_ref[...])
    gd = p[:, 3 * W + LANES:3 * W + 2 * LANES]
    g = _dot(jax.nn.sigmoid(gd), g2_ref[...])

    lw_s[...] = -math.exp(-0.5) * jax.nn.sigmoid(w0_ref[...] + wa[:, 0:W])
    a_lr = jax.nn.sigmoid(a0_ref[...] + wa[:, W:2 * W])

    gsum = gsum_ref[...]
    kk = k * kk_ref[...]
    kk = kk * lax.rsqrt(jnp.maximum(_group_sum(kk * kk, gsum), 1e-24))
    k = k * (1.0 + (a_lr - 1.0) * ka_ref[...])
    bonus = _group_sum(r * k * rk_ref[...], gsum) * v

    r_s[...] = r
    k_s[...] = k
    v_s[...] = v
    a_s[...] = -kk
    b_s[...] = kk * a_lr

    lane2 = lax.broadcasted_iota(jnp.int32, (C, LANES), 1)
    trow = lax.broadcasted_iota(jnp.int32, (C, LANES), 0)
    m0 = lane2 < RW_HEAD_DIM
    scol = jnp.where(m0, lane2, lane2 - RW_HEAD_DIM)
    strict = trow > scol
    incl = trow >= scol
    brow = lax.broadcasted_iota(jnp.int32, (LANES, LANES), 0) < RW_HEAD_DIM
    bcol = lax.broadcasted_iota(jnp.int32, (LANES, LANES), 1) < RW_HEAD_DIM
    blockmask = brow == bcol
    tril = tril_ref[...]

    def bd(x):
        z = jnp.zeros_like(x)
        return jnp.concatenate([jnp.where(m0, x, z), jnp.where(m0, z, x)], axis=0)

    n_pairs = RW_HEADS // 2
    grp = RW_GROUP
    gc = grp * C
    grow = lax.broadcasted_iota(jnp.int32, (gc, 1), 0)
    n_levels = int(math.log2(C))

    def prep_body(i, carry):
        rows = pl.ds(pl.multiple_of(i * gc, gc), gc)
        lw = lw_s[rows, :]
        h1 = lw.astype(BF16)
        r1 = lw - h1.astype(F32)
        h2 = r1.astype(BF16)
        h3 = (r1 - h2.astype(F32)).astype(BF16)
        cum = (jnp.dot(tril, h1, preferred_element_type=F32)
               + jnp.dot(tril, h2, preferred_element_type=F32)
               + jnp.dot(tril, h3, preferred_element_type=F32))
        tot = cum[C - 1:C, :]
        for g in range(1, grp):
            tot = jnp.where(grow >= g * C, cum[(g + 1) * C - 1:(g + 1) * C, :], tot)
        for g in range(grp):
            wtot_s[pl.ds(pl.multiple_of((i * grp + g) * SUBLANES, SUBLANES), SUBLANES), :] = jnp.broadcast_to(
                jnp.exp(cum[(g + 1) * C - 1:(g + 1) * C, :]), (SUBLANES, W))
        w_inv = jnp.exp(-cum)
        w_rem = jnp.exp(tot - cum)
        rc = r_s[rows, :]
        kc = k_s[rows, :]
        bc = b_s[rows, :]
        vc = v_s[rows, :].astype(BF16)
        rt32 = rc * jnp.exp(cum)
        at32 = a_s[rows, :] * jnp.exp(cum - lw)
        rt = rt32.astype(BF16)
        at = at32.astype(BF16)
        bt = (bc * w_inv).astype(BF16)
        kt = (kc * w_inv).astype(BF16)
        bh = (bc * w_rem).astype(BF16)
        kh = (kc * w_rem).astype(BF16)

        units = [(g, j) for g in range(grp) for j in range(n_pairs)]
        blk = lambda arr, g, j: arr[g * C:(g + 1) * C, LANES * j:LANES * (j + 1)]
        zero = jnp.zeros((C, LANES), F32)
        a_rbk, z, nmat, bdv = {}, {}, {}, {}
        for u in units:
            lhs = jnp.concatenate([blk(at, *u), blk(rt, *u)], axis=0)
            gram = _dot_nt(lhs, jnp.concatenate([bd(blk(bt, *u)), bd(blk(kt, *u))], axis=0))
            nmat[u] = jnp.where(strict, gram[0:C, 0:LANES], zero).astype(BF16)
            z[u] = jnp.where(strict, gram[0:C, LANES:2 * LANES], zero)
            a_rbk[u] = jnp.concatenate([jnp.where(incl, gram[C:2 * C, 0:LANES], zero),
                                        jnp.where(incl, gram[C:2 * C, LANES:2 * LANES], zero)],
                                       axis=1).astype(BF16)
            bdv[u] = bd(blk(vc, *u))
        for u in units:
            z[u] = jnp.concatenate([blk(at32, *u), _dot(z[u], bdv[u])], axis=1)
        for lvl in range(n_levels):
            last = lvl == n_levels - 1
            for u in units:
                zb = z[u].astype(BF16)
                rhs = [bd(zb[:, 0:LANES]), bd(zb[:, LANES:2 * LANES])] + ([] if last else [bd(nmat[u])])
                comb = _dot(nmat[u], jnp.concatenate(rhs, axis=1))
                z[u] = z[u] + comb[:, 0:2 * LANES]
                if not last:
                    nmat[u] = comb[:, 2 * LANES:3 * LANES].astype(BF16)
        zblock = jnp.zeros((LANES, LANES), BF16)
        for u in units:
            g, j = u
            crow = pl.ds(pl.multiple_of(i * gc + g * C, C), C)
            sl = slice(LANES * j, LANES * (j + 1))
            idx = (i * grp + g) * n_pairs + j
            zb = z[u].astype(BF16)
            apb = zb[:, 0:LANES]
            uvb = zb[:, LANES:2 * LANES]
            rhs = jnp.concatenate([jnp.concatenate([bd(apb), bd(uvb)], axis=1),
                                   jnp.concatenate([zblock, bdv[u]], axis=1)], axis=0)
            out = jnp.dot(a_rbk[u], rhs, preferred_element_type=F32)
            rp_s[crow, sl] = blk(rt32, *u) + out[:, 0:LANES]
            yv_s[crow, sl] = out[:, LANES:2 * LANES]
            x = _dot_tn(apb, blk(bh, *u))
            x_s[idx] = jnp.where(blockmask, x, jnp.zeros_like(x)).astype(BF16)
            e = _dot_tn(jnp.concatenate([uvb, blk(vc, *u)], axis=0),
                        jnp.concatenate([blk(bh, *u), blk(kh, *u)], axis=0))
            e_s[idx] = jnp.where(blockmask, e, jnp.zeros_like(e))
        return carry

    lax.fori_loop(0, tt // gc, prep_body, 0)

    def scan_body(c, carry):
        rows = pl.ds(pl.multiple_of(c * C, C), C)
        w_tot = wtot_s[pl.ds(pl.multiple_of(c * SUBLANES, SUBLANES), SUBLANES), :][0:1, :]
        for j in range(n_pairs):
            sl = slice(LANES * j, LANES * (j + 1))
            s_old = state_s[j]
            sb = s_old.astype(BF16)
            y_s[rows, sl] = _dot_nt(rp_s[rows, sl], sb) + yv_s[rows, sl]
            state_s[j] = (s_old * w_tot[:, sl] + e_s[c * n_pairs + j]
                          + jnp.dot(sb, x_s[c * n_pairs + j], preferred_element_type=F32))
        return carry

    lax.fori_loop(0, tt // C, scan_body, 0)

    y = y_s[...]
    inv_n = 1.0 / RW_HEAD_DIM
    mean = _group_sum(y, gsum) * inv_n
    yc = y - mean
    var = _group_sum(yc * yc, gsum) * inv_n
    yn = yc * lax.rsqrt(var + RW_GN_EPS) * lnw_ref[...] + lnb_ref[...]
    o_ref[0] = ((yn + bonus) * g).astype(o_ref.dtype)


def _rwkv_mixer(p_rw, mu, w0, wa, a0, g2, k_k, k_a, r_k, lnw, lnb, tt):
    b, t, _ = p_rw.shape
    gsum = jnp.asarray(np.kron(np.eye(RW_HEADS), np.ones((RW_HEAD_DIM, RW_HEAD_DIM))), BF16)
    tril = jnp.asarray(np.kron(np.eye(RW_GROUP), np.tril(np.ones((RW_CHUNK, RW_CHUNK)))), BF16)
    n_chunks = tt // RW_CHUNK
    n_pairs = RW_HEADS // 2
    vec = lambda: _const_spec((1, RW_WIDTH))
    seq_scratch = lambda: pltpu.VMEM((tt, RW_WIDTH), F32)
    return pl.pallas_call(
        _rwkv_kernel,
        grid=(b, t // tt),
        in_specs=[pl.BlockSpec((1, tt, RW_COLS), lambda i, j: (i, j, 0)),
                  _const_spec((1, RW_COLS)), vec(), _const_spec(wa.shape), vec(),
                  _const_spec(g2.shape), vec(), vec(), vec(), vec(), vec(),
                  _const_spec(gsum.shape), _const_spec(tril.shape)],
        out_specs=pl.BlockSpec((1, tt, RW_WIDTH), lambda i, j: (i, j, 0)),
        out_shape=jax.ShapeDtypeStruct((b, t, RW_WIDTH), BF16),
        scratch_shapes=[seq_scratch() for _ in range(9)]
        + [pltpu.VMEM((n_chunks * n_pairs, LANES, LANES), BF16),
           pltpu.VMEM((n_chunks * n_pairs, LANES, LANES), F32),
           pltpu.VMEM((n_chunks * SUBLANES, RW_WIDTH), F32),
           pltpu.VMEM((n_pairs, LANES, LANES), F32),
           pltpu.VMEM((RW_COLS // LANES, SUBLANES + tt, LANES), F32)],
        compiler_params=pltpu.CompilerParams(dimension_semantics=("arbitrary", "arbitrary"),
                                             vmem_limit_bytes=VMEM_LIMIT),
        name="rwkv7_mixer",
    )(p_rw, mu, w0, wa, a0, g2, k_k, k_a, r_k, lnw, lnb, gsum, tril)


def _retention_kernel(p_ref, pos_ref, ones_ref, o_ref, q_s, k_s, o_s, dmask_s, state_s):
    tt = p_ref.shape[1]
    C = RET_CHUNK
    QW = RET_QK_WIDTH
    half = RET_QK_DIM // 2
    n_pairs = RET_HEADS // 2

    @pl.when(pl.program_id(1) == 0)
    def _():
        state_s[...] = jnp.zeros_like(state_s)

    lane = lax.broadcasted_iota(jnp.int32, (1, LANES), 1)
    freq = (lane & (half - 1)).astype(F32)
    inv = jnp.exp(freq * (-math.log(ROPE_BASE) / half))
    first = (lane & (RET_QK_DIM - 1)) < half
    n_copies = LANES // half
    rp = tt // n_copies
    lane_blk = lane >> int(math.log2(half))
    pos = pos_ref[0].astype(F32)
    ang = pos[0:rp, :] * inv
    for g in range(1, n_copies):
        ang = jnp.where(lane_blk == g, pos[g * rp:(g + 1) * rp, :] * inv, ang)

    def spread(tab):
        rolled = [tab] + [pltpu.roll(tab, half * k, 1) for k in range(1, n_copies)]
        groups = []
        for g in range(n_copies):
            full = rolled[(0 - g) % n_copies]
            for blk in range(1, n_copies):
                full = jnp.where(lane_blk == blk, rolled[(blk - g) % n_copies], full)
            groups.append(full)
        return jnp.concatenate(groups, axis=0)

    cos = spread(jnp.cos(ang))
    sin = spread(jnp.sin(ang))
    sin = jnp.where(first, -sin, sin)

    def rope(z):
        swapped = jnp.where(first, pltpu.roll(z, LANES - half, 1), pltpu.roll(z, half, 1))
        return z * cos + swapped * sin

    for j in range(n_pairs):
        sl = slice(LANES * j, LANES * (j + 1))
        q_s[:, sl] = rope(p_ref[0, :, sl].astype(F32))
        k_s[:, sl] = rope(p_ref[0, :, QW + LANES * j:QW + LANES * (j + 1)].astype(F32)) * (RET_QK_DIM ** -0.5)

    lane2 = lax.broadcasted_iota(jnp.int32, (C, 2 * LANES), 1)
    trow2 = lax.broadcasted_iota(jnp.int32, (C, 2 * LANES), 0)
    lane1 = lax.broadcasted_iota(jnp.int32, (C, LANES), 1)
    trow1 = lax.broadcasted_iota(jnp.int32, (C, LANES), 0).astype(F32)
    m0 = lane1 < RET_QK_DIM
    q_decay, k_decay, chunk_decay = [], [], []
    for j in range(n_pairs):
        lg = lambda hsel: jnp.log1p(-jnp.exp2(-5.0 - (2 * j + hsel).astype(F32)))
        lg2 = lg(lane2 >> 7)
        rel = (trow2 - (lane2 & (LANES - 1))).astype(F32)
        dmask_s[j] = jnp.where(rel >= 0, jnp.exp(jnp.maximum(rel, 0.0) * lg2), 0.0)
        lg1 = lg(lane1 >> 6)
        q_decay.append(jnp.exp((trow1 + 1.0) * lg1))
        k_decay.append(jnp.exp((C - 1.0 - trow1) * lg1))
        srow = lax.broadcasted_iota(jnp.int32, (LANES, LANES), 0) >> 6
        chunk_decay.append(jnp.exp(C * lg(srow)))

    def bd(x):
        z = jnp.zeros_like(x)
        return jnp.concatenate([jnp.where(m0, x, z), jnp.where(m0, z, x)], axis=0)

    units = [(c, j) for c in range(tt // C) for j in range(n_pairs)]
    hrow = lax.broadcasted_iota(jnp.int32, (LANES, LANES), 0) < RET_QK_DIM
    vcols = lambda j: slice(2 * QW + 2 * LANES * j, 2 * QW + 2 * LANES * (j + 1))
    scores, incr, qd = {}, {}, {}
    for c, j in units:
        rows = slice(c * C, (c + 1) * C)
        sl = slice(LANES * j, LANES * (j + 1))
        qj = q_s[rows, sl]
        kj = k_s[rows, sl]
        scores[c, j] = (_dot_nt(qj, bd(kj.astype(BF16))) * dmask_s[j]).astype(BF16)
        qd[c, j] = (qj * q_decay[j]).astype(BF16)
        x = _dot_tn(kj * k_decay[j], p_ref[0, rows, vcols(j)])
        incr[c, j] = jnp.where(hrow, x[:, 0:LANES], x[:, LANES:2 * LANES])
    state = {}
    for j in range(n_pairs):
        r_cur = state_s[j]
        for c in range(tt // C):
            state[c, j] = r_cur.astype(BF16)
            r_cur = r_cur * chunk_decay[j] + incr[c, j]
        state_s[j] = r_cur
    for c, j in units:
        rows = slice(c * C, (c + 1) * C)
        vpair = p_ref[0, rows, vcols(j)]
        zq = jnp.zeros_like(qd[c, j])
        for hh in range(2):
            h = 2 * j + hh
            qm = jnp.where(m0, qd[c, j], zq) if hh == 0 else jnp.where(m0, zq, qd[c, j])
            lhs = jnp.concatenate([scores[c, j][:, LANES * hh:LANES * (hh + 1)], qm], axis=1)
            rhs = jnp.concatenate([vpair[:, LANES * hh:LANES * (hh + 1)], state[c, j]], axis=0)
            o_s[rows, LANES * h:LANES * (h + 1)] = jnp.dot(lhs, rhs, preferred_element_type=F32)

    ones = ones_ref[...]
    inv_n = 1.0 / RET_V_DIM
    for h in range(RET_HEADS):
        sl = slice(LANES * h, LANES * (h + 1))
        o = o_s[:, sl]
        mean = _group_sum(o, ones) * inv_n
        oc = o - mean
        var = _group_sum(oc * oc, ones) * inv_n
        gate = p_ref[0, :, 2 * QW + RET_V_WIDTH + LANES * h:2 * QW + RET_V_WIDTH + LANES * (h + 1)].astype(F32)
        o_ref[0, :, sl] = (gate * jax.nn.sigmoid(gate) * oc * lax.rsqrt(var + RET_GN_EPS)).astype(o_ref.dtype)


def _retention_mixer(p_ret, pos3, tt):
    b, t, _ = p_ret.shape
    ones = jnp.ones((LANES, LANES), BF16)
    n_pairs = RET_HEADS // 2
    return pl.pallas_call(
        _retention_kernel,
        grid=(b, t // tt),
        in_specs=[pl.BlockSpec((1, tt, RET_COLS), lambda i, j: (i, j, 0)),
                  pl.BlockSpec((1, tt, 1), lambda i, j: (i, j, 0)),
                  _const_spec(ones.shape)],
        out_specs=pl.BlockSpec((1, tt, RET_V_WIDTH), lambda i, j: (i, j, 0)),
        out_shape=jax.ShapeDtypeStruct((b, t, RET_V_WIDTH), BF16),
        scratch_shapes=[pltpu.VMEM((tt, RET_QK_WIDTH), F32), pltpu.VMEM((tt, RET_QK_WIDTH), F32),
                        pltpu.VMEM((tt, RET_V_WIDTH), F32),
                        pltpu.VMEM((n_pairs, RET_CHUNK, 2 * LANES), F32),
                        pltpu.VMEM((n_pairs, LANES, LANES), F32)],
        compiler_params=pltpu.CompilerParams(dimension_semantics=("arbitrary", "arbitrary"),
                                             vmem_limit_bytes=VMEM_LIMIT),
        name="retention_mixer",
    )(p_ret, pos3, ones)


def _merge_kernel(x_ref, yrw_ref, yret_ref, pg_ref, wbrw_ref, wbret_ref, wout_ref,
                  npost_ref, nffn_ref, h_ref, hn_ref):
    g_rw = jax.nn.sigmoid(pg_ref[:, 0:D_MODEL].astype(F32))
    g_ret = jax.nn.sigmoid(pg_ref[:, D_MODEL:2 * D_MODEL].astype(F32))
    merged = (g_rw * jnp.dot(yrw_ref[...], wbrw_ref[...], preferred_element_type=F32)
              + g_ret * jnp.dot(yret_ref[...], wbret_ref[...], preferred_element_type=F32))
    mixed = _dot(merged, wout_ref[...])
    h = x_ref[...] + _rms_norm(mixed, npost_ref[...])
    h_ref[...] = h
    hn_ref[...] = _rms_norm(h, nffn_ref[...]).astype(BF16)


def _merge(x2, y_rw, y_ret, p_gate, wb_rw, wb_ret, w_out, n_post, n_ffn, tm):
    n = x2.shape[0]
    row = lambda w: pl.BlockSpec((tm, w), lambda i: (i, 0))
    return pl.pallas_call(
        _merge_kernel,
        grid=(n // tm,),
        in_specs=[row(D_MODEL), row(RW_WIDTH), row(RET_V_WIDTH), row(2 * D_MODEL),
                  _const_spec(wb_rw.shape), _const_spec(wb_ret.shape), _const_spec(w_out.shape),
                  _const_spec((1, D_MODEL)), _const_spec((1, D_MODEL))],
        out_specs=[row(D_MODEL), row(D_MODEL)],
        out_shape=[jax.ShapeDtypeStruct((n, D_MODEL), F32),
                   jax.ShapeDtypeStruct((n, D_MODEL), BF16)],
        compiler_params=pltpu.CompilerParams(dimension_semantics=("arbitrary",),
                                             vmem_limit_bytes=VMEM_LIMIT),
        name="merge_out_projection",
    )(x2, y_rw, y_ret, p_gate, wb_rw, wb_ret, w_out, n_post, n_ffn)


def _gelu_tanh(x):
    k0 = -2.0 * math.sqrt(2.0 / math.pi) * math.log2(math.e)
    k1 = k0 * 0.044715
    e = jnp.exp2(x * (k1 * (x * x) + k0))
    return x * (1.0 / (1.0 + e))


def _ffn_kernel(hn_ref, h_ref, wup_ref, cw_ref, cb_ref, wdn_ref, npost_ref, o_ref, slab_s, acc_s):
    tt = hn_ref.shape[1]
    fc = FF_CHUNK
    first_tile = pl.program_id(1) == 0

    hn = hn_ref[0]

    def conv(u, col):
        outs = []
        for blk in range(fc // LANES):
            c0 = col + blk * LANES
            slab = slab_s.at[c0 // LANES]
            prev = jnp.where(first_tile, 0.0, slab[tt:tt + SUBLANES, :])
            slab[0:SUBLANES, :] = prev
            ub = u[:, blk * LANES:(blk + 1) * LANES]
            slab[SUBLANES:SUBLANES + tt, :] = ub
            s1 = slab[SUBLANES - 1:SUBLANES - 1 + tt, :]
            s2 = slab[SUBLANES - 2:SUBLANES - 2 + tt, :]
            cw = cw_ref[:, c0:c0 + LANES]
            outs.append(ub * cw[2:3, :] + s1 * cw[1:2, :] + s2 * cw[0:1, :] + cb_ref[:, c0:c0 + LANES])
        return jnp.concatenate(outs, axis=1)

    def up(j):
        cg = j * fc
        cv = D_FF + j * fc
        return (jnp.dot(hn, wup_ref[:, cg:cg + fc], preferred_element_type=F32),
                jnp.dot(hn, wup_ref[:, cv:cv + fc], preferred_element_type=F32))

    n_chunks = D_FF // fc
    u_next = up(0)
    acts = []
    for j in range(n_chunks):
        cg = j * fc
        cv = D_FF + j * fc
        u_gate, u_val = u_next
        if j + 1 < n_chunks:
            u_next = up(j + 1)
        gate = conv(u_gate, cg)
        val = conv(u_val, cv)
        acts.append((_gelu_tanh(gate) * val).astype(BF16))
        if len(acts) == FF_DOWN_GROUP or j == n_chunks - 1:
            k0 = (j + 1 - len(acts)) * fc
            act = acts[0] if len(acts) == 1 else jnp.concatenate(acts, axis=1)
            part = jnp.dot(act, wdn_ref[k0:k0 + len(acts) * fc, :], preferred_element_type=F32)
            if k0 == 0:
                acc_s[...] = part
            else:
                acc_s[...] += part
            acts = []

    o_ref[0] = h_ref[0] + _rms_norm(acc_s[...], npost_ref[...])


def _conv_ffn(hn3, h3, w_up, conv_w, conv_b, w_down, n_post, tt):
    b, t, _ = hn3.shape
    tile = lambda: pl.BlockSpec((1, tt, D_MODEL), lambda i, j: (i, j, 0))
    return pl.pallas_call(
        _ffn_kernel,
        grid=(b, t // tt),
        in_specs=[tile(), tile(), _const_spec(w_up.shape), _const_spec(conv_w.shape),
                  _const_spec(conv_b.shape), _const_spec(w_down.shape), _const_spec((1, D_MODEL))],
        out_specs=tile(),
        out_shape=jax.ShapeDtypeStruct((b, t, D_MODEL), F32),
        scratch_shapes=[pltpu.VMEM((2 * D_FF // LANES, SUBLANES + tt, LANES), F32),
                        pltpu.VMEM((tt, D_MODEL), F32)],
        compiler_params=pltpu.CompilerParams(dimension_semantics=("arbitrary", "arbitrary"),
                                             vmem_limit_bytes=VMEM_LIMIT),
        name="conv_ffn",
    )(hn3, h3, w_up, conv_w, conv_b, w_down, n_post)


def kernel(x, positions, norm_mix_pre, norm_mix_post, norm_ffn_pre, norm_ffn_post, w_in, rw_mu, rw_w0, rw_w2, rw_a0, rw_a2, rw_g2, rw_k_k, rw_k_a, rw_r_k, rw_lnx_w, rw_lnx_b, w_branch_rw, w_branch_ret, w_out, ffn_w_up, ffn_conv_w, ffn_conv_b, ffn_w_down):
    b, t, d = x.shape
    assert d == D_MODEL and norm_mix_pre.shape[0] == 1
    n = b * t
    tm = min(TOKEN_TILE, n)
    tt = min(SEQ_TILE, t)
    assert n % tm == 0 and t % tt == 0 and tt % RET_CHUNK == 0
    vec = lambda a: a[0].reshape(1, -1).astype(F32)

    w_in_b = w_in[0].astype(BF16)
    w_rw = w_in_b[:, 0:RW_COLS]
    w_ret = w_in_b[:, RW_COLS:RW_COLS + RET_COLS]
    w_gate = w_in_b[:, RW_COLS + RET_COLS:]
    zeros = jnp.zeros((RW_DECAY_RANK, RW_WIDTH), F32)
    wa = jnp.concatenate([jnp.concatenate([rw_w2[0], zeros], axis=1),
                          jnp.concatenate([zeros, rw_a2[0]], axis=1)], axis=0).astype(BF16)

    x2 = x.reshape(n, d)
    p_rw, p_ret, p_gate = _input_projection(x2, vec(norm_mix_pre), w_rw, w_ret, w_gate, tm)

    y_rw = _rwkv_mixer(p_rw.reshape(b, t, RW_COLS), vec(rw_mu), vec(rw_w0), wa, vec(rw_a0),
                       rw_g2[0].astype(BF16), vec(rw_k_k), vec(rw_k_a), vec(rw_r_k),
                       vec(rw_lnx_w), vec(rw_lnx_b), tt)
    y_ret = _retention_mixer(p_ret.reshape(b, t, RET_COLS), positions.reshape(b, t, 1), tt)

    h, hn = _merge(x2, y_rw.reshape(n, RW_WIDTH), y_ret.reshape(n, RET_V_WIDTH), p_gate,
                   w_branch_rw[0].astype(BF16), w_branch_ret[0].astype(BF16), w_out[0].astype(BF16),
                   vec(norm_mix_post), vec(norm_ffn_pre), tm)

    out = _conv_ffn(hn.reshape(b, t, d), h.reshape(b, t, d), ffn_w_up[0].astype(BF16),
                    ffn_conv_w[0].astype(F32), vec(ffn_conv_b), ffn_w_down[0].astype(BF16),
                    vec(norm_ffn_post), tt)
    return out.astype(x.dtype)
```

```python
import functools
import math

import numpy as np
import jax
import jax.numpy as jnp
from jax import lax
from jax.experimental import pallas as pl
from jax.experimental.pallas import tpu as pltpu

F32 = jnp.float32
BF16 = jnp.bfloat16

D_MODEL = 1024
RW_HEAD_DIM = 64
RW_WIDTH = 512
RW_HEADS = 8
RW_DECAY_RANK = 64
RW_ICLR_RANK = 64
RW_GATE_RANK = 128
RW_COLS = 3 * RW_WIDTH + RW_DECAY_RANK + RW_ICLR_RANK + RW_GATE_RANK
RW_GN_EPS = 64e-5
RW_CHUNK = 64
RW_GROUP = 8
RW_CUM_CHUNKS = 2
RET_QK_DIM = 64
RET_QK_WIDTH = 512
RET_HEADS = 8
RET_V_DIM = 128
RET_V_WIDTH = 1024
RET_COLS = 2 * RET_QK_WIDTH + 2 * RET_V_WIDTH
RET_CHUNK = 128
ROPE_BASE = 10000.0
RET_GN_EPS = 1e-5
D_FF = 2816
CONV_WIDTH = 3
RMS_EPS = 1e-6

LANES = 128
SUBLANES = 8
MXU_DIM = 256
VMEM_LIMIT = 56 * 1024 * 1024
TOKEN_TILE = 512
SEQ_TILE = 512
FF_CHUNK = 256
FF_DOWN_GROUP = 11


def _dot(a, b):
    return jnp.dot(a.astype(BF16), b.astype(BF16), preferred_element_type=F32)


def _dot_nt(a, b):
    return lax.dot_general(a.astype(BF16), b.astype(BF16), (((1,), (1,)), ((), ())),
                           preferred_element_type=F32)


def _dot_tn(a, b):
    return lax.dot_general(a.astype(BF16), b.astype(BF16), (((0,), (0,)), ((), ())),
                           preferred_element_type=F32)


def _group_sum(x, m):
    xb = x.astype(BF16)
    bw = m.shape[0]
    parts = [jnp.dot(xb[:, c:c + bw], m, preferred_element_type=F32) for c in range(0, x.shape[1], bw)]
    return parts[0] if len(parts) == 1 else jnp.concatenate(parts, axis=1)


def _rms_norm(x, g):
    ms = jnp.mean(x * x, axis=-1, keepdims=True)
    return x * lax.rsqrt(ms + RMS_EPS) * g


def _const_spec(shape):
    nd = len(shape)
    return pl.BlockSpec(shape, lambda *_: (0,) * nd, pipeline_mode=pl.Buffered(1))


def _proj_kernel(x_ref, g_ref, wrw_ref, wret_ref, wg_ref, prw_ref, pret_ref, pg_ref):
    hn = _rms_norm(x_ref[...], g_ref[...]).astype(BF16)
    prw_ref[...] = jnp.dot(hn, wrw_ref[...], preferred_element_type=F32).astype(BF16)
    pret_ref[...] = jnp.dot(hn, wret_ref[...], preferred_element_type=F32).astype(BF16)
    pg_ref[...] = jnp.dot(hn, wg_ref[...], preferred_element_type=F32).astype(BF16)


def _input_projection(x2, g, w_rw, w_ret, w_gate, tm):
    n = x2.shape[0]
    row = lambda w: pl.BlockSpec((tm, w), lambda i: (i, 0))
    return pl.pallas_call(
        _proj_kernel,
        grid=(n // tm,),
        in_specs=[row(D_MODEL), _const_spec((1, D_MODEL)), _const_spec(w_rw.shape),
                  _const_spec(w_ret.shape), _const_spec(w_gate.shape)],
        out_specs=[row(RW_COLS), row(RET_COLS), row(2 * D_MODEL)],
        out_shape=[jax.ShapeDtypeStruct((n, RW_COLS), BF16),
                   jax.ShapeDtypeStruct((n, RET_COLS), BF16),
                   jax.ShapeDtypeStruct((n, 2 * D_MODEL), BF16)],
        compiler_params=pltpu.CompilerParams(dimension_semantics=("arbitrary",),
                                             vmem_limit_bytes=VMEM_LIMIT),
        name="input_projection",
    )(x2, g, w_rw, w_ret, w_gate)


def _rwkv_kernel(p_ref, mu_ref, w0_ref, wa_ref, a0_ref, g2_ref, kk_ref, ka_ref, rk_ref,
                 lnw_ref, lnb_ref, gsum_ref, tril_ref, o_ref,
                 r_s, k_s, v_s, a_s, b_s, lw_s, y_s, state_s, slab_s):
    tt = p_ref.shape[1]
    W = RW_WIDTH
    C = RW_CHUNK

    @pl.when(pl.program_id(1) == 0)
    def _():
        state_s[...] = jnp.zeros_like(state_s)

    first_tile = pl.program_id(1) == 0
    blocks = []
    for blk in range(RW_COLS // LANES):
        sl = slice(blk * LANES, (blk + 1) * LANES)
        slab = slab_s.at[blk]
        slab[0:SUBLANES, :] = jnp.where(first_tile, 0.0, slab[tt:tt + SUBLANES, :])
        pb = p_ref[0, :, sl].astype(F32)
        slab[SUBLANES:SUBLANES + tt, :] = pb
        blocks.append(pb + (slab[SUBLANES - 1:SUBLANES - 1 + tt, :] - pb) * mu_ref[:, sl])
    p = jnp.concatenate(blocks, axis=1)

    r = p[:, 0:W]
    k = p[:, W:2 * W]
    v = p[:, 2 * W:3 * W]
    lane = lax.broadcasted_iota(jnp.int32, (1, LANES), 1)
    wa_in = p[:, 3 * W:3 * W + LANES]
    wa_in = jnp.where(lane < RW_DECAY_RANK, jnp.tanh(wa_in), wa_in)
    wa = _dot(wa_in, wa_ref[...])
    gd = p[:, 3 * W + LANES:3 * W + 2 * LANES]
    g = _dot(jax.nn.sigmoid(gd), g2_ref[...])

    lw_s[...] = -math.exp(-0.5) * jax.nn.sigmoid(w0_ref[...] + wa[:, 0:W])
    a_lr = jax.nn.sigmoid(a0_ref[...] + wa[:, W:2 * W])

    gsum = gsum_ref[...]
    kk = k * kk_ref[...]
    kk = kk * lax.rsqrt(jnp.maximum(_group_sum(kk * kk, gsum), 1e-24))
    k = k * (1.0 + (a_lr - 1.0) * ka_ref[...])
    bonus = _group_sum(r * k * rk_ref[...], gsum) * v

    r_s[...] = r
    k_s[...] = k
    v_s[...] = v
    a_s[...] = -kk
    b_s[...] = kk * a_lr

    lane2 = lax.broadcasted_iota(jnp.int32, (C, LANES), 1)
    trow = lax.broadcasted_iota(jnp.int32, (C, LANES), 0)
    m0 = lane2 < RW_HEAD_DIM
    scol = jnp.where(m0, lane2, lane2 - RW_HEAD_DIM)
    strict = trow > scol
    incl = trow >= scol
    brow = lax.broadcasted_iota(jnp.int32, (LANES, LANES), 0) < RW_HEAD_DIM
    bcol = lax.broadcasted_iota(jnp.int32, (LANES, LANES), 1) < RW_HEAD_DIM
    blockmask = brow == bcol
    tril = tril_ref[...]

    def bd(x):
        z = jnp.zeros_like(x)
        return jnp.concatenate([jnp.where(m0, x, z), jnp.where(m0, z, x)], axis=0)

    n_pairs = RW_HEADS // 2
    grp = min(RW_GROUP, tt // C)
    gc = grp * C
    grow = lax.broadcasted_iota(jnp.int32, (gc, 1), 0)
    n_levels = int(math.log2(C))

    def prep_body(i, carry):
        rows = pl.ds(pl.multiple_of(i * gc, gc), gc)
        lw = lw_s[rows, :]
        h1 = lw.astype(BF16)
        h2 = (lw - h1.astype(F32)).astype(BF16)
        cb = tril.shape[0]
        cum = jnp.concatenate(
            [jnp.dot(tril, h1[q * cb:(q + 1) * cb], preferred_element_type=F32)
             + jnp.dot(tril, h2[q * cb:(q + 1) * cb], preferred_element_type=F32)
             for q in range(gc // cb)], axis=0)
        tot = cum[C - 1:C, :]
        for g in range(1, grp):
            tot = jnp.where(grow >= g * C, cum[(g + 1) * C - 1:(g + 1) * C, :], tot)
        w_inv = jnp.exp(-cum)
        w_rem = jnp.exp(tot - cum)
        rc = r_s[rows, :]
        kc = k_s[rows, :]
        bc = b_s[rows, :]
        v32c = v_s[rows, :]
        vc = v32c.astype(BF16)
        rt32 = rc * jnp.exp(cum)
        at32 = a_s[rows, :] * jnp.exp(cum - lw)
        rt = rt32.astype(BF16)
        at = at32.astype(BF16)
        bt = (bc * w_inv).astype(BF16)
        kt = (kc * w_inv).astype(BF16)
        bh = (bc * w_rem).astype(BF16)
        kh = (kc * w_rem).astype(BF16)

        units = [(g, j) for g in range(grp) for j in range(n_pairs)]
        blk = lambda arr, g, j: arr[g * C:(g + 1) * C, LANES * j:LANES * (j + 1)]
        zero = jnp.zeros((C, LANES), F32)
        stack2 = lambda x: jnp.concatenate([x, x], axis=0)
        zero2 = jnp.zeros((2 * C, LANES), F32)
        a_rb, a_rk, z, n_bd, bdv = {}, {}, {}, {}, {}
        ak_bd, vw, a_st = {}, {}, {}
        for u in units:
            lhs = jnp.concatenate([blk(at, *u), blk(rt, *u)], axis=0)
            gram = _dot_nt(lhs, jnp.concatenate([bd(blk(bt, *u)), bd(blk(kt, *u))], axis=0))
            a_ab = jnp.where(strict, gram[0:C, 0:LANES], zero)
            a_ak = jnp.where(strict, gram[0:C, LANES:2 * LANES], zero)
            n_bd[u] = jnp.where(blockmask, stack2(a_ab), zero2).astype(BF16)
            ak_bd[u] = jnp.where(blockmask, stack2(a_ak), zero2).astype(BF16)
            a_rb[u] = jnp.where(incl, gram[C:2 * C, 0:LANES], zero).astype(BF16)
            a_rk[u] = jnp.where(incl, gram[C:2 * C, LANES:2 * LANES], zero).astype(BF16)
            bdv[u] = bd(blk(vc, *u))
            v32 = blk(v32c, *u)
            vw[u] = jnp.concatenate([pltpu.roll(v32, RW_HEAD_DIM, 1), v32], axis=0).astype(BF16)
            a32 = blk(at32, *u)
            a_st[u] = jnp.concatenate([a32, pltpu.roll(a32, RW_HEAD_DIM, 1)], axis=0)
        for u in units:
            akv = jnp.dot(ak_bd[u], vw[u], preferred_element_type=F32)
            z[u] = jnp.where(bcol, a_st[u], akv)
        for lvl in range(n_levels):
            last = lvl == n_levels - 1
            for u in units:
                zb = z[u].astype(BF16)
                rhs = zb if last else jnp.concatenate([zb, n_bd[u]], axis=1)
                comb = jnp.dot(n_bd[u], rhs, preferred_element_type=F32)
                z[u] = z[u] + comb[:, 0:LANES]
                if not last:
                    n_bd[u] = comb[:, LANES:2 * LANES].astype(BF16)
        def finish(u):
            top = z[u][0:C]
            bot = z[u][C:2 * C]
            apb = jnp.where(m0, top, pltpu.roll(bot, RW_HEAD_DIM, 1)).astype(BF16)
            uvb = jnp.where(m0, pltpu.roll(top, RW_HEAD_DIM, 1), bot).astype(BF16)
            out = jnp.dot(a_rb[u], jnp.concatenate([bd(apb), bd(uvb)], axis=1),
                          preferred_element_type=F32)
            rp = (blk(rt32, *u) + out[:, 0:LANES]).astype(BF16)
            yv = out[:, LANES:2 * LANES] + jnp.dot(a_rk[u], bdv[u], preferred_element_type=F32)
            x = _dot_tn(apb, blk(bh, *u))
            x = jnp.where(blockmask, x, jnp.zeros_like(x)).astype(BF16)
            e = _dot_tn(jnp.concatenate([uvb, blk(vc, *u)], axis=0),
                        jnp.concatenate([blk(bh, *u), blk(kh, *u)], axis=0))
            e = jnp.where(blockmask, e, jnp.zeros_like(e))
            return rp, yv, x, e

        s_cur = [state_s[j] for j in range(n_pairs)]

        def scan_step(g, fin):
            crow = pl.ds(pl.multiple_of(i * gc + g * C, C), C)
            w_tot = jnp.exp(cum[(g + 1) * C - 1:(g + 1) * C, :])
            for j in range(n_pairs):
                sl = slice(LANES * j, LANES * (j + 1))
                rp, yv, x, e = fin[j]
                sb = s_cur[j].astype(BF16)
                y_s[crow, sl] = _dot_nt(rp, sb) + yv
                s_cur[j] = s_cur[j] * w_tot[:, sl] + e + jnp.dot(sb, x, preferred_element_type=F32)

        pending = None
        for g in range(grp):
            fin = [finish((g, j)) for j in range(n_pairs)]
            if pending is not None:
                scan_step(*pending)
            pending = (g, fin)
        scan_step(*pending)
        for j in range(n_pairs):
            state_s[j] = s_cur[j]
        return carry

    lax.fori_loop(0, tt // gc, prep_body, 0)

    y = y_s[...]
    inv_n = 1.0 / RW_HEAD_DIM
    mean = _group_sum(y, gsum) * inv_n
    yc = y - mean
    var = _group_sum(yc * yc, gsum) * inv_n
    yn = yc * lax.rsqrt(var + RW_GN_EPS) * lnw_ref[...] + lnb_ref[...]
    o_ref[0] = ((yn + bonus) * g).astype(o_ref.dtype)


def _rwkv_mixer(p_rw, mu, w0, wa, a0, g2, k_k, k_a, r_k, lnw, lnb, tt):
    b, t, _ = p_rw.shape
    gsum = jnp.asarray(np.kron(np.eye(MXU_DIM // RW_HEAD_DIM), np.ones((RW_HEAD_DIM, RW_HEAD_DIM))), BF16)
    tril = jnp.asarray(np.kron(np.eye(RW_CUM_CHUNKS), np.tril(np.ones((RW_CHUNK, RW_CHUNK)))), BF16)
    n_pairs = RW_HEADS // 2
    vec = lambda: _const_spec((1, RW_WIDTH))
    seq_scratch = lambda: pltpu.VMEM((tt, RW_WIDTH), F32)
    return pl.pallas_call(
        _rwkv_kernel,
        grid=(b, t // tt),
        in_specs=[pl.BlockSpec((1, tt, RW_COLS), lambda i, j: (i, j, 0)),
                  _const_spec((1, RW_COLS)), vec(), _const_spec(wa.shape), vec(),
                  _const_spec(g2.shape), vec(), vec(), vec(), vec(), vec(),
                  _const_spec(gsum.shape), _const_spec(tril.shape)],
        out_specs=pl.BlockSpec((1, tt, RW_WIDTH), lambda i, j: (i, j, 0)),
        out_shape=jax.ShapeDtypeStruct((b, t, RW_WIDTH), BF16),
        scratch_shapes=[seq_scratch() for _ in range(7)]
        + [pltpu.VMEM((n_pairs, LANES, LANES), F32),
           pltpu.VMEM((RW_COLS // LANES, SUBLANES + tt, LANES), F32)],
        compiler_params=pltpu.CompilerParams(dimension_semantics=("arbitrary", "arbitrary"),
                                             vmem_limit_bytes=VMEM_LIMIT),
        name="rwkv7_mixer",
    )(p_rw, mu, w0, wa, a0, g2, k_k, k_a, r_k, lnw, lnb, gsum, tril)


def _retention_kernel(p_ref, pos_ref, ones_ref, o_ref, q_s, k_s, o_s, dmask_s, state_s):
    tt = p_ref.shape[1]
    C = RET_CHUNK
    QW = RET_QK_WIDTH
    half = RET_QK_DIM // 2
    n_pairs = RET_HEADS // 2

    @pl.when(pl.program_id(1) == 0)
    def _():
        state_s[...] = jnp.zeros_like(state_s)

    lane = lax.broadcasted_iota(jnp.int32, (1, LANES), 1)
    freq = (lane & (half - 1)).astype(F32)
    inv = jnp.exp(freq * (-math.log(ROPE_BASE) / half))
    first = (lane & (RET_QK_DIM - 1)) < half
    n_copies = LANES // half
    rp = tt // n_copies
    lane_blk = lane >> int(math.log2(half))
    pos = pos_ref[0].astype(F32)
    ang = pos[0:rp, :] * inv
    for g in range(1, n_copies):
        ang = jnp.where(lane_blk == g, pos[g * rp:(g + 1) * rp, :] * inv, ang)

    def spread(tab):
        rolled = [tab] + [pltpu.roll(tab, half * k, 1) for k in range(1, n_copies)]
        groups = []
        for g in range(n_copies):
            full = rolled[(0 - g) % n_copies]
            for blk in range(1, n_copies):
                full = jnp.where(lane_blk == blk, rolled[(blk - g) % n_copies], full)
            groups.append(full)
        return jnp.concatenate(groups, axis=0)

    cos = spread(jnp.cos(ang))
    sin = spread(jnp.sin(ang))
    sin = jnp.where(first, -sin, sin)

    def rope(z):
        swapped = jnp.where(first, pltpu.roll(z, LANES - half, 1), pltpu.roll(z, half, 1))
        return z * cos + swapped * sin

    for j in range(n_pairs):
        sl = slice(LANES * j, LANES * (j + 1))
        q_s[:, sl] = rope(p_ref[0, :, sl].astype(F32))
        k_s[:, sl] = rope(p_ref[0, :, QW + LANES * j:QW + LANES * (j + 1)].astype(F32)) * (RET_QK_DIM ** -0.5)

    lane2 = lax.broadcasted_iota(jnp.int32, (C, 2 * LANES), 1)
    trow2 = lax.broadcasted_iota(jnp.int32, (C, 2 * LANES), 0)
    lane1 = lax.broadcasted_iota(jnp.int32, (C, LANES), 1)
    trow1 = lax.broadcasted_iota(jnp.int32, (C, LANES), 0).astype(F32)
    m0 = lane1 < RET_QK_DIM
    q_decay, k_decay, chunk_decay = [], [], []
    for j in range(n_pairs):
        lg = lambda hsel: jnp.log1p(-jnp.exp2(-5.0 - (2 * j + hsel).astype(F32)))
        lg2 = lg(lane2 >> 7)
        rel = (trow2 - (lane2 & (LANES - 1))).astype(F32)
        dmask_s[j] = jnp.where(rel >= 0, jnp.exp(jnp.maximum(rel, 0.0) * lg2), 0.0)
        lg1 = lg(lane1 >> 6)
        q_decay.append(jnp.exp((trow1 + 1.0) * lg1))
        k_decay.append(jnp.exp((C - 1.0 - trow1) * lg1))
        srow = lax.broadcasted_iota(jnp.int32, (LANES, LANES), 0) >> 6
        chunk_decay.append(jnp.exp(C * lg(srow)))

    def bd(x):
        z = jnp.zeros_like(x)
        return jnp.concatenate([jnp.where(m0, x, z), jnp.where(m0, z, x)], axis=0)

    units = [(c, j) for c in range(tt // C) for j in range(n_pairs)]
    hrow = lax.broadcasted_iota(jnp.int32, (LANES, LANES), 0) < RET_QK_DIM
    vcols = lambda j: slice(2 * QW + 2 * LANES * j, 2 * QW + 2 * LANES * (j + 1))
    scores, incr, qd = {}, {}, {}
    for c, j in units:
        rows = slice(c * C, (c + 1) * C)
        sl = slice(LANES * j, LANES * (j + 1))
        qj = q_s[rows, sl]
        kj = k_s[rows, sl]
        scores[c, j] = (_dot_nt(qj, bd(kj.astype(BF16))) * dmask_s[j]).astype(BF16)
        qd[c, j] = (qj * q_decay[j]).astype(BF16)
        x = _dot_tn(kj * k_decay[j], p_ref[0, rows, vcols(j)])
        incr[c, j] = jnp.where(hrow, x[:, 0:LANES], x[:, LANES:2 * LANES])
    state = {}
    for j in range(n_pairs):
        r_cur = state_s[j]
        for c in range(tt // C):
            state[c, j] = r_cur.astype(BF16)
            r_cur = r_cur * chunk_decay[j] + incr[c, j]
        state_s[j] = r_cur
    for c, j in units:
        rows = slice(c * C, (c + 1) * C)
        vpair = p_ref[0, rows, vcols(j)]
        zq = jnp.zeros_like(qd[c, j])
        for hh in range(2):
            h = 2 * j + hh
            qm = jnp.where(m0, qd[c, j], zq) if hh == 0 else jnp.where(m0, zq, qd[c, j])
            lhs = jnp.concatenate([scores[c, j][:, LANES * hh:LANES * (hh + 1)], qm], axis=1)
            rhs = jnp.concatenate([vpair[:, LANES * hh:LANES * (hh + 1)], state[c, j]], axis=0)
            o_s[rows, LANES * h:LANES * (h + 1)] = jnp.dot(lhs, rhs, preferred_element_type=F32)

    ones = ones_ref[...]
    inv_n = 1.0 / RET_V_DIM
    gate0 = 2 * QW + RET_V_WIDTH
    for c0 in range(0, RET_V_WIDTH, MXU_DIM):
        sl = slice(c0, c0 + MXU_DIM)
        o = o_s[:, sl]
        mean = _group_sum(o, ones) * inv_n
        oc = o - mean
        var = _group_sum(oc * oc, ones) * inv_n
        gate = p_ref[0, :, gate0 + c0:gate0 + c0 + MXU_DIM].astype(F32)
        o_ref[0, :, sl] = (gate * jax.nn.sigmoid(gate) * oc * lax.rsqrt(var + RET_GN_EPS)).astype(o_ref.dtype)


def _retention_mixer(p_ret, pos3, tt):
    b, t, _ = p_ret.shape
    ones = jnp.asarray(np.kron(np.eye(MXU_DIM // RET_V_DIM), np.ones((RET_V_DIM, RET_V_DIM))), BF16)
    n_pairs = RET_HEADS // 2
    return pl.pallas_call(
        _retention_kernel,
        grid=(b, t // tt),
        in_specs=[pl.BlockSpec((1, tt, RET_COLS), lambda i, j: (i, j, 0)),
                  pl.BlockSpec((1, tt, 1), lambda i, j: (i, j, 0)),
                  _const_spec(ones.shape)],
        out_specs=pl.BlockSpec((1, tt, RET_V_WIDTH), lambda i, j: (i, j, 0)),
        out_shape=jax.ShapeDtypeStruct((b, t, RET_V_WIDTH), BF16),
        scratch_shapes=[pltpu.VMEM((tt, RET_QK_WIDTH), F32), pltpu.VMEM((tt, RET_QK_WIDTH), F32),
                        pltpu.VMEM((tt, RET_V_WIDTH), F32),
                        pltpu.VMEM((n_pairs, RET_CHUNK, 2 * LANES), F32),
                        pltpu.VMEM((n_pairs, LANES, LANES), F32)],
        compiler_params=pltpu.CompilerParams(dimension_semantics=("arbitrary", "arbitrary"),
                                             vmem_limit_bytes=VMEM_LIMIT),
        name="retention_mixer",
    )(p_ret, pos3, ones)


def _merge_kernel(x_ref, yrw_ref, yret_ref, pg_ref, wbrw_ref, wbret_ref, wout_ref,
                  npost_ref, nffn_ref, h_ref, hn_ref):
    g_rw = jax.nn.sigmoid(pg_ref[:, 0:D_MODEL].astype(F32))
    g_ret = jax.nn.sigmoid(pg_ref[:, D_MODEL:2 * D_MODEL].astype(F32))
    merged = (g_rw * jnp.dot(yrw_ref[...], wbrw_ref[...], preferred_element_type=F32)
              + g_ret * jnp.dot(yret_ref[...], wbret_ref[...], preferred_element_type=F32))
    mixed = _dot(merged, wout_ref[...])
    h = x_ref[...] + _rms_norm(mixed, npost_ref[...])
    h_ref[...] = h
    hn_ref[...] = _rms_norm(h, nffn_ref[...]).astype(BF16)


def _merge(x2, y_rw, y_ret, p_gate, wb_rw, wb_ret, w_out, n_post, n_ffn, tm):
    n = x2.shape[0]
    row = lambda w: pl.BlockSpec((tm, w), lambda i: (i, 0))
    return pl.pallas_call(
        _merge_kernel,
        grid=(n // tm,),
        in_specs=[row(D_MODEL), row(RW_WIDTH), row(RET_V_WIDTH), row(2 * D_MODEL),
                  _const_spec(wb_rw.shape), _const_spec(wb_ret.shape), _const_spec(w_out.shape),
                  _const_spec((1, D_MODEL)), _const_spec((1, D_MODEL))],
        out_specs=[row(D_MODEL), row(D_MODEL)],
        out_shape=[jax.ShapeDtypeStruct((n, D_MODEL), F32),
                   jax.ShapeDtypeStruct((n, D_MODEL), BF16)],
        compiler_params=pltpu.CompilerParams(dimension_semantics=("arbitrary",),
                                             vmem_limit_bytes=VMEM_LIMIT),
        name="merge_out_projection",
    )(x2, y_rw, y_ret, p_gate, wb_rw, wb_ret, w_out, n_post, n_ffn)


def _gelu_tanh(x):
    k0 = -2.0 * math.sqrt(2.0 / math.pi) * math.log2(math.e)
    k1 = k0 * 0.044715
    e = jnp.exp2(x * (k1 * (x * x) + k0))
    return x * (1.0 / (1.0 + e))


def _ffn_kernel(hn_ref, h_ref, wup_ref, cw_ref, cb_ref, wdn_ref, npost_ref, o_ref, slab_s, acc_s):
    tt = hn_ref.shape[1]
    fc = FF_CHUNK
    first_tile = pl.program_id(1) == 0

    hn = hn_ref[0]

    def conv(u, col):
        outs = []
        for blk in range(fc // LANES):
            c0 = col + blk * LANES
            slab = slab_s.at[c0 // LANES]
            prev = jnp.where(first_tile, 0.0, slab[tt:tt + SUBLANES, :])
            slab[0:SUBLANES, :] = prev
            ub = u[:, blk * LANES:(blk + 1) * LANES]
            slab[SUBLANES:SUBLANES + tt, :] = ub
            s1 = slab[SUBLANES - 1:SUBLANES - 1 + tt, :]
            s2 = slab[SUBLANES - 2:SUBLANES - 2 + tt, :]
            cw = cw_ref[:, c0:c0 + LANES]
            outs.append(ub * cw[2:3, :] + s1 * cw[1:2, :] + s2 * cw[0:1, :] + cb_ref[:, c0:c0 + LANES])
        return jnp.concatenate(outs, axis=1)

    def up(j):
        cg = j * fc
        cv = D_FF + j * fc
        return (jnp.dot(hn, wup_ref[:, cg:cg + fc], preferred_element_type=F32),
                jnp.dot(hn, wup_ref[:, cv:cv + fc], preferred_element_type=F32))

    n_chunks = D_FF // fc
    u_next = up(0)
    acts = []
    for j in range(n_chunks):
        cg = j * fc
        cv = D_FF + j * fc
        u_gate, u_val = u_next
        if j + 1 < n_chunks:
            u_next = up(j + 1)
        gate = conv(u_gate, cg)
        val = conv(u_val, cv)
        acts.append((_gelu_tanh(gate) * val).astype(BF16))
        if len(acts) == FF_DOWN_GROUP or j == n_chunks - 1:
            k0 = (j + 1 - len(acts)) * fc
            act = acts[0] if len(acts) == 1 else jnp.concatenate(acts, axis=1)
            part = jnp.dot(act, wdn_ref[k0:k0 + len(acts) * fc, :], preferred_element_type=F32)
            if k0 == 0:
                acc_s[...] = part
            else:
                acc_s[...] += part
            acts = []

    o_ref[0] = h_ref[0] + _rms_norm(acc_s[...], npost_ref[...])


def _conv_ffn(hn3, h3, w_up, conv_w, conv_b, w_down, n_post, tt):
    b, t, _ = hn3.shape
    tile = lambda: pl.BlockSpec((1, tt, D_MODEL), lambda i, j: (i, j, 0))
    return pl.pallas_call(
        _ffn_kernel,
        grid=(b, t // tt),
        in_specs=[tile(), tile(), _const_spec(w_up.shape), _const_spec(conv_w.shape),
                  _const_spec(conv_b.shape), _const_spec(w_down.shape), _const_spec((1, D_MODEL))],
        out_specs=tile(),
        out_shape=jax.ShapeDtypeStruct((b, t, D_MODEL), F32),
        scratch_shapes=[pltpu.VMEM((2 * D_FF // LANES, SUBLANES + tt, LANES), F32),
                        pltpu.VMEM((tt, D_MODEL), F32)],
        compiler_params=pltpu.CompilerParams(dimension_semantics=("arbitrary", "arbitrary"),
                                             vmem_limit_bytes=VMEM_LIMIT),
        name="conv_ffn",
    )(hn3, h3, w_up, conv_w, conv_b, w_down, n_post)


def kernel(x, positions, norm_mix_pre, norm_mix_post, norm_ffn_pre, norm_ffn_post, w_in, rw_mu, rw_w0, rw_w2, rw_a0, rw_a2, rw_g2, rw_k_k, rw_k_a, rw_r_k, rw_lnx_w, rw_lnx_b, w_branch_rw, w_branch_ret, w_out, ffn_w_up, ffn_conv_w, ffn_conv_b, ffn_w_down):
    b, t, d = x.shape
    assert d == D_MODEL and norm_mix_pre.shape[0] == 1
    n = b * t
    tm = min(TOKEN_TILE, n)
    tt = min(SEQ_TILE, t)
    assert n % tm == 0 and t % tt == 0 and tt % RET_CHUNK == 0
    vec = lambda a: a[0].reshape(1, -1).astype(F32)

    w_in_b = w_in[0].astype(BF16)
    w_rw = w_in_b[:, 0:RW_COLS]
    w_ret = w_in_b[:, RW_COLS:RW_COLS + RET_COLS]
    w_gate = w_in_b[:, RW_COLS + RET_COLS:]
    zeros = jnp.zeros((RW_DECAY_RANK, RW_WIDTH), F32)
    wa = jnp.concatenate([jnp.concatenate([rw_w2[0], zeros], axis=1),
                          jnp.concatenate([zeros, rw_a2[0]], axis=1)], axis=0).astype(BF16)

    x2 = x.reshape(n, d)
    p_rw, p_ret, p_gate = _input_projection(x2, vec(norm_mix_pre), w_rw, w_ret, w_gate, tm)

    y_rw = _rwkv_mixer(p_rw.reshape(b, t, RW_COLS), vec(rw_mu), vec(rw_w0), wa, vec(rw_a0),
                       rw_g2[0].astype(BF16), vec(rw_k_k), vec(rw_k_a), vec(rw_r_k),
                       vec(rw_lnx_w), vec(rw_lnx_b), tt)
    y_ret = _retention_mixer(p_ret.reshape(b, t, RET_COLS), positions.reshape(b, t, 1), tt)

    h, hn = _merge(x2, y_rw.reshape(n, RW_WIDTH), y_ret.reshape(n, RET_V_WIDTH), p_gate,
                   w_branch_rw[0].astype(BF16), w_branch_ret[0].astype(BF16), w_out[0].astype(BF16),
                   vec(norm_mix_post), vec(norm_ffn_pre), tm)

    out = _conv_ffn(hn.reshape(b, t, d), h.reshape(b, t, d), ffn_w_up[0].astype(BF16),
                    ffn_conv_w[0].astype(F32), vec(ffn_conv_b), ffn_w_down[0].astype(BF16),
                    vec(norm_ffn_post), tt)
    return out.astype(x.dtype)
```

```python
import functools
import math

import numpy as np
import jax
import jax.numpy as jnp
from jax import lax
from jax.experimental import pallas as pl
from jax.experimental.pallas import tpu as pltpu

F32 = jnp.float32
BF16 = jnp.bfloat16

D_MODEL = 1024
RW_HEAD_DIM = 64
RW_WIDTH = 512
RW_HEADS = 8
RW_DECAY_RANK = 64
RW_ICLR_RANK = 64
RW_GATE_RANK = 128
RW_COLS = 3 * RW_WIDTH + RW_DECAY_RANK + RW_ICLR_RANK + RW_GATE_RANK
RW_GN_EPS = 64e-5
RW_CHUNK = 64
RW_GROUP = 8
RW_CUM_CHUNKS = 2
RET_QK_DIM = 64
RET_QK_WIDTH = 512
RET_HEADS = 8
RET_V_DIM = 128
RET_V_WIDTH = 1024
RET_COLS = 2 * RET_QK_WIDTH + 2 * RET_V_WIDTH
RET_CHUNK = 128
ROPE_BASE = 10000.0
RET_GN_EPS = 1e-5
D_FF = 2816
CONV_WIDTH = 3
RMS_EPS = 1e-6

LANES = 128
SUBLANES = 8
MXU_DIM = 256
VMEM_LIMIT = 56 * 1024 * 1024
TOKEN_TILE = 512
SEQ_TILE = 512
MERGE_SPLIT = 4
FF_CHUNK = 256
FF_DOWN_GROUP = 11


def _dot(a, b):
    return jnp.dot(a.astype(BF16), b.astype(BF16), preferred_element_type=F32)


def _dot_nt(a, b):
    return lax.dot_general(a.astype(BF16), b.astype(BF16), (((1,), (1,)), ((), ())),
                           preferred_element_type=F32)


def _dot_tn(a, b):
    return lax.dot_general(a.astype(BF16), b.astype(BF16), (((0,), (0,)), ((), ())),
                           preferred_element_type=F32)


def _group_sum(x, m):
    xb = x.astype(BF16)
    bw = m.shape[0]
    parts = [jnp.dot(xb[:, c:c + bw], m, preferred_element_type=F32) for c in range(0, x.shape[1], bw)]
    return parts[0] if len(parts) == 1 else jnp.concatenate(parts, axis=1)


def _rms_norm(x, g):
    ms = jnp.mean(x * x, axis=-1, keepdims=True)
    return x * lax.rsqrt(ms + RMS_EPS) * g


def _const_spec(shape):
    nd = len(shape)
    return pl.BlockSpec(shape, lambda *_: (0,) * nd, pipeline_mode=pl.Buffered(1))


def _rwkv_kernel(p_ref, mu_ref, w0_ref, wa_ref, a0_ref, g2_ref, kk_ref, ka_ref, rk_ref,
                 lnw_ref, lnb_ref, gsum_ref, tril_ref, o_ref,
                 r_s, k_s, v_s, a_s, b_s, lw_s, y_s, state_s, slab_s):
    tt = p_ref.shape[1]
    W = RW_WIDTH
    C = RW_CHUNK

    @pl.when(pl.program_id(1) == 0)
    def _():
        state_s[...] = jnp.zeros_like(state_s)

    first_tile = pl.program_id(1) == 0
    blocks = []
    for blk in range(RW_COLS // LANES):
        sl = slice(blk * LANES, (blk + 1) * LANES)
        slab = slab_s.at[blk]
        slab[0:SUBLANES, :] = jnp.where(first_tile, 0.0, slab[tt:tt + SUBLANES, :])
        pb = p_ref[0, :, sl].astype(F32)
        slab[SUBLANES:SUBLANES + tt, :] = pb
        blocks.append(pb + (slab[SUBLANES - 1:SUBLANES - 1 + tt, :] - pb) * mu_ref[:, sl])
    p = jnp.concatenate(blocks, axis=1)

    r = p[:, 0:W]
    k = p[:, W:2 * W]
    v = p[:, 2 * W:3 * W]
    lane = lax.broadcasted_iota(jnp.int32, (1, LANES), 1)
    wa_in = p[:, 3 * W:3 * W + LANES]
    wa_in = jnp.where(lane < RW_DECAY_RANK, jnp.tanh(wa_in), wa_in)
    wa = _dot(wa_in, wa_ref[...])
    gd = p[:, 3 * W + LANES:3 * W + 2 * LANES]
    g = _dot(jax.nn.sigmoid(gd), g2_ref[...])

    lw_s[...] = -math.exp(-0.5) * jax.nn.sigmoid(w0_ref[...] + wa[:, 0:W])
    a_lr = jax.nn.sigmoid(a0_ref[...] + wa[:, W:2 * W])

    gsum = gsum_ref[...]
    kk = k * kk_ref[...]
    kk = kk * lax.rsqrt(jnp.maximum(_group_sum(kk * kk, gsum), 1e-24))
    k = k * (1.0 + (a_lr - 1.0) * ka_ref[...])
    bonus = _group_sum(r * k * rk_ref[...], gsum) * v

    r_s[...] = r
    k_s[...] = k
    v_s[...] = v
    a_s[...] = -kk
    b_s[...] = kk * a_lr

    lane2 = lax.broadcasted_iota(jnp.int32, (C, LANES), 1)
    trow = lax.broadcasted_iota(jnp.int32, (C, LANES), 0)
    m0 = lane2 < RW_HEAD_DIM
    scol = jnp.where(m0, lane2, lane2 - RW_HEAD_DIM)
    strict = trow > scol
    incl = trow >= scol
    brow = lax.broadcasted_iota(jnp.int32, (LANES, LANES), 0) < RW_HEAD_DIM
    bcol = lax.broadcasted_iota(jnp.int32, (LANES, LANES), 1) < RW_HEAD_DIM
    blockmask = brow == bcol
    tril = tril_ref[...]

    def bd(x):
        z = jnp.zeros_like(x)
        return jnp.concatenate([jnp.where(m0, x, z), jnp.where(m0, z, x)], axis=0)

    n_pairs = RW_HEADS // 2
    grp = min(RW_GROUP, tt // C)
    gc = grp * C
    grow = lax.broadcasted_iota(jnp.int32, (gc, 1), 0)
    n_levels = int(math.log2(C))

    def prep_body(i, carry):
        rows = pl.ds(pl.multiple_of(i * gc, gc), gc)
        lw = lw_s[rows, :]
        h1 = lw.astype(BF16)
        h2 = (lw - h1.astype(F32)).astype(BF16)
        cb = tril.shape[0]
        cum = jnp.concatenate(
            [jnp.dot(tril, h1[q * cb:(q + 1) * cb], preferred_element_type=F32)
             + jnp.dot(tril, h2[q * cb:(q + 1) * cb], preferred_element_type=F32)
             for q in range(gc // cb)], axis=0)
        tot = cum[C - 1:C, :]
        for g in range(1, grp):
            tot = jnp.where(grow >= g * C, cum[(g + 1) * C - 1:(g + 1) * C, :], tot)
        w_inv = jnp.exp(-cum)
        w_rem = jnp.exp(tot - cum)
        rc = r_s[rows, :]
        kc = k_s[rows, :]
        bc = b_s[rows, :]
        v32c = v_s[rows, :]
        vc = v32c.astype(BF16)
        rt32 = rc * jnp.exp(cum)
        at32 = a_s[rows, :] * jnp.exp(cum - lw)
        rt = rt32.astype(BF16)
        at = at32.astype(BF16)
        bt = (bc * w_inv).astype(BF16)
        kt = (kc * w_inv).astype(BF16)
        bh = (bc * w_rem).astype(BF16)
        kh = (kc * w_rem).astype(BF16)

        units = [(g, j) for g in range(grp) for j in range(n_pairs)]
        blk = lambda arr, g, j: arr[g * C:(g + 1) * C, LANES * j:LANES * (j + 1)]
        zero = jnp.zeros((C, LANES), F32)
        stack2 = lambda x: jnp.concatenate([x, x], axis=0)
        zero2 = jnp.zeros((2 * C, LANES), F32)
        a_rb, a_rk, z, n_bd, bdv = {}, {}, {}, {}, {}
        ak_bd, vw, a_st = {}, {}, {}
        for u in units:
            lhs = jnp.concatenate([blk(at, *u), blk(rt, *u)], axis=0)
            gram = _dot_nt(lhs, jnp.concatenate([bd(blk(bt, *u)), bd(blk(kt, *u))], axis=0))
            a_ab = jnp.where(strict, gram[0:C, 0:LANES], zero)
            a_ak = jnp.where(strict, gram[0:C, LANES:2 * LANES], zero)
            n_bd[u] = jnp.where(blockmask, stack2(a_ab), zero2).astype(BF16)
            ak_bd[u] = jnp.where(blockmask, stack2(a_ak), zero2).astype(BF16)
            a_rb[u] = jnp.where(incl, gram[C:2 * C, 0:LANES], zero).astype(BF16)
            a_rk[u] = jnp.where(incl, gram[C:2 * C, LANES:2 * LANES], zero).astype(BF16)
            bdv[u] = bd(blk(vc, *u))
            v32 = blk(v32c, *u)
            vw[u] = jnp.concatenate([pltpu.roll(v32, RW_HEAD_DIM, 1), v32], axis=0).astype(BF16)
            a32 = blk(at32, *u)
            a_st[u] = jnp.concatenate([a32, pltpu.roll(a32, RW_HEAD_DIM, 1)], axis=0)
        for u in units:
            akv = jnp.dot(ak_bd[u], vw[u], preferred_element_type=F32)
            z[u] = jnp.where(bcol, a_st[u], akv)
        for lvl in range(n_levels):
            last = lvl == n_levels - 1
            for u in units:
                zb = z[u].astype(BF16)
                rhs = zb if last else jnp.concatenate([zb, n_bd[u]], axis=1)
                comb = jnp.dot(n_bd[u], rhs, preferred_element_type=F32)
                z[u] = z[u] + comb[:, 0:LANES]
                if not last:
                    n_bd[u] = comb[:, LANES:2 * LANES].astype(BF16)
        def finish(u):
            top = z[u][0:C]
            bot = z[u][C:2 * C]
            apb = jnp.where(m0, top, pltpu.roll(bot, RW_HEAD_DIM, 1)).astype(BF16)
            uvb = jnp.where(m0, pltpu.roll(top, RW_HEAD_DIM, 1), bot).astype(BF16)
            out = jnp.dot(a_rb[u], jnp.concatenate([bd(apb), bd(uvb)], axis=1),
                          preferred_element_type=F32)
            rp = (blk(rt32, *u) + out[:, 0:LANES]).astype(BF16)
            yv = out[:, LANES:2 * LANES] + jnp.dot(a_rk[u], bdv[u], preferred_element_type=F32)
            x = _dot_tn(apb, blk(bh, *u))
            x = jnp.where(blockmask, x, jnp.zeros_like(x)).astype(BF16)
            e = _dot_tn(jnp.concatenate([uvb, blk(vc, *u)], axis=0),
                        jnp.concatenate([blk(bh, *u), blk(kh, *u)], axis=0))
            e = jnp.where(blockmask, e, jnp.zeros_like(e))
            return rp, yv, x, e

        s_cur = [state_s[j] for j in range(n_pairs)]

        def scan_step(g, fin):
            crow = pl.ds(pl.multiple_of(i * gc + g * C, C), C)
            w_tot = jnp.exp(cum[(g + 1) * C - 1:(g + 1) * C, :])
            for j in range(n_pairs):
                sl = slice(LANES * j, LANES * (j + 1))
                rp, yv, x, e = fin[j]
                sb = s_cur[j].astype(BF16)
                y_s[crow, sl] = _dot_nt(rp, sb) + yv
                s_cur[j] = s_cur[j] * w_tot[:, sl] + e + jnp.dot(sb, x, preferred_element_type=F32)

        pending = None
        for g in range(grp):
            fin = [finish((g, j)) for j in range(n_pairs)]
            if pending is not None:
                scan_step(*pending)
            pending = (g, fin)
        scan_step(*pending)
        for j in range(n_pairs):
            state_s[j] = s_cur[j]
        return carry

    lax.fori_loop(0, tt // gc, prep_body, 0)

    y = y_s[...]
    inv_n = 1.0 / RW_HEAD_DIM
    mean = _group_sum(y, gsum) * inv_n
    yc = y - mean
    var = _group_sum(yc * yc, gsum) * inv_n
    yn = yc * lax.rsqrt(var + RW_GN_EPS) * lnw_ref[...] + lnb_ref[...]
    o_ref[0] = ((yn + bonus) * g).astype(o_ref.dtype)


def _rwkv_mixer(p_rw, mu, w0, wa, a0, g2, k_k, k_a, r_k, lnw, lnb, tt):
    b, t, _ = p_rw.shape
    gsum = jnp.asarray(np.kron(np.eye(MXU_DIM // RW_HEAD_DIM), np.ones((RW_HEAD_DIM, RW_HEAD_DIM))), BF16)
    tril = jnp.asarray(np.kron(np.eye(RW_CUM_CHUNKS), np.tril(np.ones((RW_CHUNK, RW_CHUNK)))), BF16)
    n_pairs = RW_HEADS // 2
    vec = lambda: _const_spec((1, RW_WIDTH))
    seq_scratch = lambda: pltpu.VMEM((tt, RW_WIDTH), F32)
    return pl.pallas_call(
        _rwkv_kernel,
        grid=(b, t // tt),
        in_specs=[pl.BlockSpec((1, tt, RW_COLS), lambda i, j: (i, j, 0)),
                  _const_spec((1, RW_COLS)), vec(), _const_spec(wa.shape), vec(),
                  _const_spec(g2.shape), vec(), vec(), vec(), vec(), vec(),
                  _const_spec(gsum.shape), _const_spec(tril.shape)],
        out_specs=pl.BlockSpec((1, tt, RW_WIDTH), lambda i, j: (i, j, 0)),
        out_shape=jax.ShapeDtypeStruct((b, t, RW_WIDTH), BF16),
        scratch_shapes=[seq_scratch() for _ in range(7)]
        + [pltpu.VMEM((n_pairs, LANES, LANES), F32),
           pltpu.VMEM((RW_COLS // LANES, SUBLANES + tt, LANES), F32)],
        compiler_params=pltpu.CompilerParams(dimension_semantics=("arbitrary", "arbitrary"),
                                             vmem_limit_bytes=VMEM_LIMIT),
        name="rwkv7_mixer",
    )(p_rw, mu, w0, wa, a0, g2, k_k, k_a, r_k, lnw, lnb, gsum, tril)


def _proj_retention_kernel(tiles_per_row, x_ref, g_ref, wrw_ref, wret_ref, wg_ref, pos_ref, ones_ref,
                           prw_ref, pg_ref, o_ref, p_s, vg_s, q_s, k_s, o_s, dmask_s, state_s):
    tt = x_ref.shape[0]
    C = RET_CHUNK
    QW = RET_QK_WIDTH
    half = RET_QK_DIM // 2
    n_pairs = RET_HEADS // 2
    step = pl.program_id(0)

    @pl.when(step == 0)
    def _():
        p_s[...] = jnp.zeros_like(p_s)

    @pl.when((lax.rem(step, tiles_per_row) == 1 % tiles_per_row) | (step == 0))
    def _():
        state_s[...] = jnp.zeros_like(state_s)

    hn = _rms_norm(x_ref[...], g_ref[...]).astype(BF16)
    vg_s[...] = p_s[:, 2 * QW:]

    lane = lax.broadcasted_iota(jnp.int32, (1, LANES), 1)
    freq = (lane & (half - 1)).astype(F32)
    inv = jnp.exp(freq * (-math.log(ROPE_BASE) / half))
    first = (lane & (RET_QK_DIM - 1)) < half
    n_copies = LANES // half
    rp = tt // n_copies
    lane_blk = lane >> int(math.log2(half))
    pos = pos_ref[...].astype(F32)
    ang = pos[0:rp, :] * inv
    for g in range(1, n_copies):
        ang = jnp.where(lane_blk == g, pos[g * rp:(g + 1) * rp, :] * inv, ang)

    def spread(tab):
        rolled = [tab] + [pltpu.roll(tab, half * k, 1) for k in range(1, n_copies)]
        groups = []
        for g in range(n_copies):
            full = rolled[(0 - g) % n_copies]
            for blk in range(1, n_copies):
                full = jnp.where(lane_blk == blk, rolled[(blk - g) % n_copies], full)
            groups.append(full)
        return jnp.concatenate(groups, axis=0)

    cos = spread(jnp.cos(ang))
    sin = spread(jnp.sin(ang))
    sin = jnp.where(first, -sin, sin)

    def rope(z):
        swapped = jnp.where(first, pltpu.roll(z, LANES - half, 1), pltpu.roll(z, half, 1))
        return z * cos + swapped * sin

    for j in range(n_pairs):
        sl = slice(LANES * j, LANES * (j + 1))
        q_s[:, sl] = rope(p_s[:, sl].astype(F32))
        k_s[:, sl] = rope(p_s[:, QW + LANES * j:QW + LANES * (j + 1)].astype(F32)) * (RET_QK_DIM ** -0.5)

    prw_ref[...] = jnp.dot(hn, wrw_ref[...], preferred_element_type=F32).astype(BF16)

    lane2 = lax.broadcasted_iota(jnp.int32, (C, 2 * LANES), 1)
    trow2 = lax.broadcasted_iota(jnp.int32, (C, 2 * LANES), 0)
    lane1 = lax.broadcasted_iota(jnp.int32, (C, LANES), 1)
    trow1 = lax.broadcasted_iota(jnp.int32, (C, LANES), 0).astype(F32)
    m0 = lane1 < RET_QK_DIM
    q_decay, k_decay, chunk_decay = [], [], []
    for j in range(n_pairs):
        lg = lambda hsel: jnp.log1p(-jnp.exp2(-5.0 - (2 * j + hsel).astype(F32)))
        lg2 = lg(lane2 >> 7)
        rel = (trow2 - (lane2 & (LANES - 1))).astype(F32)
        dmask_s[j] = jnp.where(rel >= 0, jnp.exp(jnp.maximum(rel, 0.0) * lg2), 0.0)
        lg1 = lg(lane1 >> 6)
        q_decay.append(jnp.exp((trow1 + 1.0) * lg1))
        k_decay.append(jnp.exp((C - 1.0 - trow1) * lg1))
        srow = lax.broadcasted_iota(jnp.int32, (LANES, LANES), 0) >> 6
        chunk_decay.append(jnp.exp(C * lg(srow)))

    def bd(x):
        z = jnp.zeros_like(x)
        return jnp.concatenate([jnp.where(m0, x, z), jnp.where(m0, z, x)], axis=0)

    units = [(c, j) for c in range(tt // C) for j in range(n_pairs)]
    hrow = lax.broadcasted_iota(jnp.int32, (LANES, LANES), 0) < RET_QK_DIM
    vcols = lambda j: slice(2 * LANES * j, 2 * LANES * (j + 1))
    scores, incr, qd = {}, {}, {}
    for c, j in units:
        rows = slice(c * C, (c + 1) * C)
        sl = slice(LANES * j, LANES * (j + 1))
        qj = q_s[rows, sl]
        kj = k_s[rows, sl]
        scores[c, j] = (_dot_nt(qj, bd(kj.astype(BF16))) * dmask_s[j]).astype(BF16)
        qd[c, j] = (qj * q_decay[j]).astype(BF16)
        x = _dot_tn(kj * k_decay[j], vg_s[rows, vcols(j)])
        incr[c, j] = jnp.where(hrow, x[:, 0:LANES], x[:, LANES:2 * LANES])
    state = {}
    for j in range(n_pairs):
        r_cur = state_s[j]
        for c in range(tt // C):
            state[c, j] = r_cur.astype(BF16)
            r_cur = r_cur * chunk_decay[j] + incr[c, j]
        state_s[j] = r_cur
    pg_ref[...] = jnp.dot(hn, wg_ref[...], preferred_element_type=F32).astype(BF16)
    for c, j in units:
        rows = slice(c * C, (c + 1) * C)
        vpair = vg_s[rows, vcols(j)]
        zq = jnp.zeros_like(qd[c, j])
        for hh in range(2):
            h = 2 * j + hh
            qm = jnp.where(m0, qd[c, j], zq) if hh == 0 else jnp.where(m0, zq, qd[c, j])
            lhs = jnp.concatenate([scores[c, j][:, LANES * hh:LANES * (hh + 1)], qm], axis=1)
            rhs = jnp.concatenate([vpair[:, LANES * hh:LANES * (hh + 1)], state[c, j]], axis=0)
            o_s[rows, LANES * h:LANES * (h + 1)] = jnp.dot(lhs, rhs, preferred_element_type=F32)

    def project_ret(piece):
        cols = slice(piece * RET_COLS // 3, (piece + 1) * RET_COLS // 3)
        p_s[:, cols] = jnp.dot(hn, wret_ref[:, cols], preferred_element_type=F32).astype(BF16)

    ones = ones_ref[...]
    inv_n = 1.0 / RET_V_DIM
    blocks = [slice(c0, c0 + MXU_DIM) for c0 in range(0, RET_V_WIDTH, MXU_DIM)]
    project_ret(0)
    means = [_group_sum(o_s[:, sl], ones) * inv_n for sl in blocks]
    project_ret(1)
    centred = [o_s[:, sl] - m for sl, m in zip(blocks, means)]
    variances = [_group_sum(oc * oc, ones) * inv_n for oc in centred]
    project_ret(2)
    for sl, oc, var in zip(blocks, centred, variances):
        gate = vg_s[:, RET_V_WIDTH + sl.start:RET_V_WIDTH + sl.stop].astype(F32)
        o_ref[:, sl] = (gate * jax.nn.sigmoid(gate) * oc * lax.rsqrt(var + RET_GN_EPS)).astype(o_ref.dtype)


def _proj_retention(x2, g, w_rw, w_ret, w_gate, pos2, tt, tiles_per_row):
    n = x2.shape[0]
    n_tiles = n // tt
    ones = jnp.asarray(np.kron(np.eye(MXU_DIM // RET_V_DIM), np.ones((RET_V_DIM, RET_V_DIM))), BF16)
    n_pairs = RET_HEADS // 2
    cur = lambda w: pl.BlockSpec((tt, w), lambda i: (jnp.minimum(i, n_tiles - 1), 0))
    prev = lambda w: pl.BlockSpec((tt, w), lambda i: (jnp.maximum(i - 1, 0), 0))
    return pl.pallas_call(
        functools.partial(_proj_retention_kernel, tiles_per_row),
        grid=(n_tiles + 1,),
        in_specs=[cur(D_MODEL), _const_spec((1, D_MODEL)), _const_spec(w_rw.shape),
                  _const_spec(w_ret.shape), _const_spec(w_gate.shape), prev(1), _const_spec(ones.shape)],
        out_specs=[cur(RW_COLS), cur(2 * D_MODEL), prev(RET_V_WIDTH)],
        out_shape=[jax.ShapeDtypeStruct((n, RW_COLS), BF16),
                   jax.ShapeDtypeStruct((n, 2 * D_MODEL), BF16),
                   jax.ShapeDtypeStruct((n, RET_V_WIDTH), BF16)],
        scratch_shapes=[pltpu.VMEM((tt, RET_COLS), BF16), pltpu.VMEM((tt, 2 * RET_V_WIDTH), BF16),
                        pltpu.VMEM((tt, RET_QK_WIDTH), F32), pltpu.VMEM((tt, RET_QK_WIDTH), F32),
                        pltpu.VMEM((tt, RET_V_WIDTH), F32),
                        pltpu.VMEM((n_pairs, RET_CHUNK, 2 * LANES), F32),
                        pltpu.VMEM((n_pairs, LANES, LANES), F32)],
        compiler_params=pltpu.CompilerParams(dimension_semantics=("arbitrary",),
                                             vmem_limit_bytes=VMEM_LIMIT),
        name="projection_retention",
    )(x2, g, w_rw, w_ret, w_gate, pos2, ones)


def _merge_kernel(x_ref, yrw_ref, yret_ref, pg_ref, wbrw_ref, wbret_ref, wout_ref,
                  npost_ref, nffn_ref, h_ref, hn_ref):
    tm = x_ref.shape[0]
    subs = [slice(r, r + tm // MERGE_SPLIT) for r in range(0, tm, tm // MERGE_SPLIT)]
    branches = [(jnp.dot(yrw_ref[rs, :], wbrw_ref[...], preferred_element_type=F32),
                 jnp.dot(yret_ref[rs, :], wbret_ref[...], preferred_element_type=F32)) for rs in subs]
    def finish(rs, mx):
        h = x_ref[rs, :] + _rms_norm(mx, npost_ref[...])
        h_ref[rs, :] = h
        hn_ref[rs, :] = _rms_norm(h, nffn_ref[...]).astype(BF16)

    pending = None
    for rs, (b_rw, b_ret) in zip(subs, branches):
        g_rw = jax.nn.sigmoid(pg_ref[rs, 0:D_MODEL].astype(F32))
        g_ret = jax.nn.sigmoid(pg_ref[rs, D_MODEL:2 * D_MODEL].astype(F32))
        mx = _dot(g_rw * b_rw + g_ret * b_ret, wout_ref[...])
        if pending is not None:
            finish(*pending)
        pending = (rs, mx)
    finish(*pending)


def _merge(x2, y_rw, y_ret, p_gate, wb_rw, wb_ret, w_out, n_post, n_ffn, tm):
    n = x2.shape[0]
    row = lambda w: pl.BlockSpec((tm, w), lambda i: (i, 0))
    return pl.pallas_call(
        _merge_kernel,
        grid=(n // tm,),
        in_specs=[row(D_MODEL), row(RW_WIDTH), row(RET_V_WIDTH), row(2 * D_MODEL),
                  _const_spec(wb_rw.shape), _const_spec(wb_ret.shape), _const_spec(w_out.shape),
                  _const_spec((1, D_MODEL)), _const_spec((1, D_MODEL))],
        out_specs=[row(D_MODEL), row(D_MODEL)],
        out_shape=[jax.ShapeDtypeStruct((n, D_MODEL), F32),
                   jax.ShapeDtypeStruct((n, D_MODEL), BF16)],
        compiler_params=pltpu.CompilerParams(dimension_semantics=("arbitrary",),
                                             vmem_limit_bytes=VMEM_LIMIT),
        name="merge_out_projection",
    )(x2, y_rw, y_ret, p_gate, wb_rw, wb_ret, w_out, n_post, n_ffn)


def _gelu_tanh(x):
    k0 = -2.0 * math.sqrt(2.0 / math.pi) * math.log2(math.e)
    k1 = k0 * 0.044715
    e = jnp.exp2(x * (k1 * (x * x) + k0))
    return x * (1.0 / (1.0 + e))


def _ffn_kernel(hn_ref, h_ref, wup_ref, cw_ref, cb_ref, wdn_ref, npost_ref, o_ref, slab_s, acc_s):
    tt = hn_ref.shape[1]
    fc = FF_CHUNK
    first_tile = pl.program_id(1) == 0

    hn = hn_ref[0]

    def conv(u, col):
        outs = []
        for blk in range(fc // LANES):
            c0 = col + blk * LANES
            slab = slab_s.at[c0 // LANES]
            prev = jnp.where(first_tile, 0.0, slab[tt:tt + SUBLANES, :])
            slab[0:SUBLANES, :] = prev
            ub = u[:, blk * LANES:(blk + 1) * LANES]
            slab[SUBLANES:SUBLANES + tt, :] = ub
            s1 = slab[SUBLANES - 1:SUBLANES - 1 + tt, :]
            s2 = slab[SUBLANES - 2:SUBLANES - 2 + tt, :]
            cw = cw_ref[:, c0:c0 + LANES]
            outs.append(ub * cw[2:3, :] + s1 * cw[1:2, :] + s2 * cw[0:1, :] + cb_ref[:, c0:c0 + LANES])
        return jnp.concatenate(outs, axis=1)

    def up(j):
        cg = j * fc
        cv = D_FF + j * fc
        return (jnp.dot(hn, wup_ref[:, cg:cg + fc], preferred_element_type=F32),
                jnp.dot(hn, wup_ref[:, cv:cv + fc], preferred_element_type=F32))

    n_chunks = D_FF // fc
    u_next = up(0)
    acts = []
    for j in range(n_chunks):
        cg = j * fc
        cv = D_FF + j * fc
        u_gate, u_val = u_next
        if j + 1 < n_chunks:
            u_next = up(j + 1)
        gate = conv(u_gate, cg)
        val = conv(u_val, cv)
        acts.append((_gelu_tanh(gate) * val).astype(BF16))
        if len(acts) == FF_DOWN_GROUP or j == n_chunks - 1:
            k0 = (j + 1 - len(acts)) * fc
            act = acts[0] if len(acts) == 1 else jnp.concatenate(acts, axis=1)
            part = jnp.dot(act, wdn_ref[k0:k0 + len(acts) * fc, :], preferred_element_type=F32)
            if k0 == 0:
                acc_s[...] = part
            else:
                acc_s[...] += part
            acts = []

    o_ref[0] = h_ref[0] + _rms_norm(acc_s[...], npost_ref[...])


def _conv_ffn(hn3, h3, w_up, conv_w, conv_b, w_down, n_post, tt):
    b, t, _ = hn3.shape
    tile = lambda: pl.BlockSpec((1, tt, D_MODEL), lambda i, j: (i, j, 0))
    return pl.pallas_call(
        _ffn_kernel,
        grid=(b, t // tt),
        in_specs=[tile(), tile(), _const_spec(w_up.shape), _const_spec(conv_w.shape),
                  _const_spec(conv_b.shape), _const_spec(w_down.shape), _const_spec((1, D_MODEL))],
        out_specs=tile(),
        out_shape=jax.ShapeDtypeStruct((b, t, D_MODEL), F32),
        scratch_shapes=[pltpu.VMEM((2 * D_FF // LANES, SUBLANES + tt, LANES), F32),
                        pltpu.VMEM((tt, D_MODEL), F32)],
        compiler_params=pltpu.CompilerParams(dimension_semantics=("arbitrary", "arbitrary"),
                                             vmem_limit_bytes=VMEM_LIMIT),
        name="conv_ffn",
    )(hn3, h3, w_up, conv_w, conv_b, w_down, n_post)


def kernel(x, positions, norm_mix_pre, norm_mix_post, norm_ffn_pre, norm_ffn_post, w_in, rw_mu, rw_w0, rw_w2, rw_a0, rw_a2, rw_g2, rw_k_k, rw_k_a, rw_r_k, rw_lnx_w, rw_lnx_b, w_branch_rw, w_branch_ret, w_out, ffn_w_up, ffn_conv_w, ffn_conv_b, ffn_w_down):
    b, t, d = x.shape
    assert d == D_MODEL and norm_mix_pre.shape[0] == 1
    n = b * t
    tm = min(TOKEN_TILE, n)
    tt = min(SEQ_TILE, t)
    assert n % tm == 0 and t % tt == 0 and tt % RET_CHUNK == 0
    vec = lambda a: a[0].reshape(1, -1).astype(F32)

    w_in_b = w_in[0].astype(BF16)
    w_rw = w_in_b[:, 0:RW_COLS]
    w_ret = w_in_b[:, RW_COLS:RW_COLS + RET_COLS]
    w_gate = w_in_b[:, RW_COLS + RET_COLS:]
    zeros = jnp.zeros((RW_DECAY_RANK, RW_WIDTH), F32)
    wa = jnp.concatenate([jnp.concatenate([rw_w2[0], zeros], axis=1),
                          jnp.concatenate([zeros, rw_a2[0]], axis=1)], axis=0).astype(BF16)

    x2 = x.reshape(n, d)
    p_rw, p_gate, y_ret = _proj_retention(x2, vec(norm_mix_pre), w_rw, w_ret, w_gate,
                                          positions.reshape(n, 1), tt, t // tt)

    y_rw = _rwkv_mixer(p_rw.reshape(b, t, RW_COLS), vec(rw_mu), vec(rw_w0), wa, vec(rw_a0),
                       rw_g2[0].astype(BF16), vec(rw_k_k), vec(rw_k_a), vec(rw_r_k),
                       vec(rw_lnx_w), vec(rw_lnx_b), tt)

    h, hn = _merge(x2, y_rw.reshape(n, RW_WIDTH), y_ret, p_gate,
                   w_branch_rw[0].astype(BF16), w_branch_ret[0].astype(BF16), w_out[0].astype(BF16),
                   vec(norm_mix_post), vec(norm_ffn_pre), tm)

    out = _conv_ffn(hn.reshape(b, t, d), h.reshape(b, t, d), ffn_w_up[0].astype(BF16),
                    ffn_conv_w[0].astype(F32), vec(ffn_conv_b), ffn_w_down[0].astype(BF16),
                    vec(norm_ffn_post), tt)
    return out.astype(x.dtype)
```

```python
import functools
import math

import numpy as np
import jax
import jax.numpy as jnp
from jax import lax
from jax.experimental import pallas as pl
from jax.experimental.pallas import tpu as pltpu

F32 = jnp.float32
BF16 = jnp.bfloat16

D_MODEL = 1024
RW_HEAD_DIM = 64
RW_WIDTH = 512
RW_HEADS = 8
RW_DECAY_RANK = 64
RW_ICLR_RANK = 64
RW_GATE_RANK = 128
RW_COLS = 3 * RW_WIDTH + RW_DECAY_RANK + RW_ICLR_RANK + RW_GATE_RANK
RW_GN_EPS = 64e-5
RW_CHUNK = 64
RW_GROUP = 8
RW_CUM_CHUNKS = 2
RET_QK_DIM = 64
RET_QK_WIDTH = 512
RET_HEADS = 8
RET_V_DIM = 128
RET_V_WIDTH = 1024
RET_COLS = 2 * RET_QK_WIDTH + 2 * RET_V_WIDTH
RET_CHUNK = 128
ROPE_BASE = 10000.0
RET_GN_EPS = 1e-5
D_FF = 2816
CONV_WIDTH = 3
RMS_EPS = 1e-6

LANES = 128
SUBLANES = 8
MXU_DIM = 256
VMEM_LIMIT = 56 * 1024 * 1024
TOKEN_TILE = 1024
SEQ_TILE = 512
MERGE_SPLIT = 8
FF_CHUNK = 256
FF_DOWN_GROUP = 11


def _dot(a, b):
    return jnp.dot(a.astype(BF16), b.astype(BF16), preferred_element_type=F32)


def _dot_nt(a, b):
    return lax.dot_general(a.astype(BF16), b.astype(BF16), (((1,), (1,)), ((), ())),
                           preferred_element_type=F32)


def _dot_tn(a, b):
    return lax.dot_general(a.astype(BF16), b.astype(BF16), (((0,), (0,)), ((), ())),
                           preferred_element_type=F32)


def _group_sum(x, m):
    xb = x.astype(BF16)
    bw = m.shape[0]
    parts = [jnp.dot(xb[:, c:c + bw], m, preferred_element_type=F32) for c in range(0, x.shape[1], bw)]
    return parts[0] if len(parts) == 1 else jnp.concatenate(parts, axis=1)


def _rms_norm(x, g):
    ms = jnp.mean(x * x, axis=-1, keepdims=True)
    return x * lax.rsqrt(ms + RMS_EPS) * g


def _const_spec(shape):
    nd = len(shape)
    return pl.BlockSpec(shape, lambda *_: (0,) * nd, pipeline_mode=pl.Buffered(1))


def _rwkv_kernel(p_ref, mu_ref, w0_ref, wa_ref, a0_ref, g2_ref, kk_ref, ka_ref, rk_ref,
                 lnw_ref, lnb_ref, gsum_ref, tril_ref, o_ref,
                 r_s, k_s, v_s, a_s, b_s, lw_s, y_s, state_s, slab_s):
    tt = p_ref.shape[1]
    W = RW_WIDTH
    C = RW_CHUNK

    @pl.when(pl.program_id(1) == 0)
    def _():
        state_s[...] = jnp.zeros_like(state_s)

    first_tile = pl.program_id(1) == 0
    blocks = []
    for blk in range(RW_COLS // LANES):
        sl = slice(blk * LANES, (blk + 1) * LANES)
        slab = slab_s.at[blk]
        slab[0:SUBLANES, :] = jnp.where(first_tile, 0.0, slab[tt:tt + SUBLANES, :])
        pb = p_ref[0, :, sl].astype(F32)
        slab[SUBLANES:SUBLANES + tt, :] = pb
        blocks.append(pb + (slab[SUBLANES - 1:SUBLANES - 1 + tt, :] - pb) * mu_ref[:, sl])
    p = jnp.concatenate(blocks, axis=1)

    r = p[:, 0:W]
    k = p[:, W:2 * W]
    v = p[:, 2 * W:3 * W]
    lane = lax.broadcasted_iota(jnp.int32, (1, LANES), 1)
    wa_in = p[:, 3 * W:3 * W + LANES]
    wa_in = jnp.where(lane < RW_DECAY_RANK, jnp.tanh(wa_in), wa_in)
    wa = _dot(wa_in, wa_ref[...])
    gd = p[:, 3 * W + LANES:3 * W + 2 * LANES]

    lw_s[...] = -math.exp(-0.5) * jax.nn.sigmoid(w0_ref[...] + wa[:, 0:W])
    a_lr = jax.nn.sigmoid(a0_ref[...] + wa[:, W:2 * W])

    gsum = gsum_ref[...]
    kk = k * kk_ref[...]
    kk = kk * lax.rsqrt(jnp.maximum(_group_sum(kk * kk, gsum), 1e-24))
    k = k * (1.0 + (a_lr - 1.0) * ka_ref[...])

    r_s[...] = r
    k_s[...] = k
    v_s[...] = v
    a_s[...] = -kk
    b_s[...] = kk * a_lr

    lane2 = lax.broadcasted_iota(jnp.int32, (C, LANES), 1)
    trow = lax.broadcasted_iota(jnp.int32, (C, LANES), 0)
    m0 = lane2 < RW_HEAD_DIM
    scol = jnp.where(m0, lane2, lane2 - RW_HEAD_DIM)
    strict = trow > scol
    incl = trow >= scol
    brow = lax.broadcasted_iota(jnp.int32, (LANES, LANES), 0) < RW_HEAD_DIM
    bcol = lax.broadcasted_iota(jnp.int32, (LANES, LANES), 1) < RW_HEAD_DIM
    blockmask = brow == bcol
    tril = tril_ref[...]

    def bd(x):
        z = jnp.zeros_like(x)
        return jnp.concatenate([jnp.where(m0, x, z), jnp.where(m0, z, x)], axis=0)

    n_pairs = RW_HEADS // 2
    grp = min(RW_GROUP, tt // C)
    gc = grp * C
    grow = lax.broadcasted_iota(jnp.int32, (gc, 1), 0)
    n_levels = int(math.log2(C))

    def prep_body(i, carry):
        rows = pl.ds(pl.multiple_of(i * gc, gc), gc)
        lw = lw_s[rows, :]
        h1 = lw.astype(BF16)
        h2 = (lw - h1.astype(F32)).astype(BF16)
        cb = tril.shape[0]
        cum = jnp.concatenate(
            [jnp.dot(tril, h1[q * cb:(q + 1) * cb], preferred_element_type=F32)
             + jnp.dot(tril, h2[q * cb:(q + 1) * cb], preferred_element_type=F32)
             for q in range(gc // cb)], axis=0)
        tot = cum[C - 1:C, :]
        for g in range(1, grp):
            tot = jnp.where(grow >= g * C, cum[(g + 1) * C - 1:(g + 1) * C, :], tot)
        w_inv = jnp.exp(-cum)
        w_rem = jnp.exp(tot - cum)
        rc = r_s[rows, :]
        kc = k_s[rows, :]
        bc = b_s[rows, :]
        v32c = v_s[rows, :]
        vc = v32c.astype(BF16)
        rt32 = rc * jnp.exp(cum)
        at32 = a_s[rows, :] * jnp.exp(cum - lw)
        rt = rt32.astype(BF16)
        at = at32.astype(BF16)
        bt = (bc * w_inv).astype(BF16)
        kt = (kc * w_inv).astype(BF16)
        bh = (bc * w_rem).astype(BF16)
        kh = (kc * w_rem).astype(BF16)

        units = [(g, j) for g in range(grp) for j in range(n_pairs)]
        blk = lambda arr, g, j: arr[g * C:(g + 1) * C, LANES * j:LANES * (j + 1)]
        zero = jnp.zeros((C, LANES), F32)
        stack2 = lambda x: jnp.concatenate([x, x], axis=0)
        zero2 = jnp.zeros((2 * C, LANES), F32)
        a_rb, a_rk, z, n_bd = {}, {}, {}, {}
        ak_bd, vw, a_st = {}, {}, {}
        for u in units:
            lhs = jnp.concatenate([blk(at, *u), blk(rt, *u)], axis=0)
            gram = _dot_nt(lhs, jnp.concatenate([bd(blk(bt, *u)), bd(blk(kt, *u))], axis=0))
            a_ab = jnp.where(strict, gram[0:C, 0:LANES], zero)
            a_ak = jnp.where(strict, gram[0:C, LANES:2 * LANES], zero)
            n_bd[u] = jnp.where(blockmask, stack2(a_ab), zero2).astype(BF16)
            ak_bd[u] = jnp.where(blockmask, stack2(a_ak), zero2).astype(BF16)
            a_rb[u] = jnp.where(incl, gram[C:2 * C, 0:LANES], zero).astype(BF16)
            a_rk[u] = jnp.where(incl, gram[C:2 * C, LANES:2 * LANES], zero).astype(BF16)
            v32 = blk(v32c, *u)
            vw[u] = jnp.concatenate([pltpu.roll(v32, RW_HEAD_DIM, 1), v32], axis=0).astype(BF16)
            a32 = blk(at32, *u)
            a_st[u] = jnp.concatenate([a32, pltpu.roll(a32, RW_HEAD_DIM, 1)], axis=0)
        for u in units:
            akv = jnp.dot(ak_bd[u], vw[u], preferred_element_type=F32)
            z[u] = jnp.where(bcol, a_st[u], akv)
        for lvl in range(n_levels):
            last = lvl == n_levels - 1
            for u in units:
                zb = z[u].astype(BF16)
                rhs = zb if last else jnp.concatenate([zb, n_bd[u]], axis=1)
                comb = jnp.dot(n_bd[u], rhs, preferred_element_type=F32)
                z[u] = z[u] + comb[:, 0:LANES]
                if not last:
                    n_bd[u] = comb[:, LANES:2 * LANES].astype(BF16)

        def finish(u):
            top = z[u][0:C]
            bot = z[u][C:2 * C]
            apb = jnp.where(m0, top, pltpu.roll(bot, RW_HEAD_DIM, 1)).astype(BF16)
            uvb = jnp.where(m0, pltpu.roll(top, RW_HEAD_DIM, 1), bot).astype(BF16)
            out = jnp.dot(a_rb[u], jnp.concatenate([bd(apb), bd(uvb)], axis=1),
                          preferred_element_type=F32)
            rp = (blk(rt32, *u) + out[:, 0:LANES]).astype(BF16)
            yv = out[:, LANES:2 * LANES] + jnp.dot(a_rk[u], bd(blk(vc, *u)), preferred_element_type=F32)
            x = _dot_tn(apb, blk(bh, *u))
            x = jnp.where(blockmask, x, jnp.zeros_like(x)).astype(BF16)
            e = _dot_tn(jnp.concatenate([uvb, blk(vc, *u)], axis=0),
                        jnp.concatenate([blk(bh, *u), blk(kh, *u)], axis=0))
            e = jnp.where(blockmask, e, jnp.zeros_like(e))
            return rp, yv, x, e

        s_cur = [state_s[j] for j in range(n_pairs)]

        def scan_step(g, fin):
            crow = pl.ds(pl.multiple_of(i * gc + g * C, C), C)
            w_tot = jnp.exp(cum[(g + 1) * C - 1:(g + 1) * C, :])
            for j in range(n_pairs):
                sl = slice(LANES * j, LANES * (j + 1))
                rp, yv, x, e = fin[j]
                sb = s_cur[j].astype(BF16)
                y_s[crow, sl] = _dot_nt(rp, sb) + yv
                s_cur[j] = s_cur[j] * w_tot[:, sl] + e + jnp.dot(sb, x, preferred_element_type=F32)

        pending = None
        for g in range(grp):
            fin = [finish((g, j)) for j in range(n_pairs)]
            if pending is not None:
                scan_step(*pending)
            pending = (g, fin)
        scan_step(*pending)
        for j in range(n_pairs):
            state_s[j] = s_cur[j]
        return carry

    lax.fori_loop(0, tt // gc, prep_body, 0)

    g = _dot(jax.nn.sigmoid(gd), g2_ref[...])
    bonus = _group_sum(r_s[...] * k_s[...] * rk_ref[...], gsum) * v_s[...]
    y = y_s[...]
    inv_n = 1.0 / RW_HEAD_DIM
    mean = _group_sum(y, gsum) * inv_n
    yc = y - mean
    var = _group_sum(yc * yc, gsum) * inv_n
    yn = yc * lax.rsqrt(var + RW_GN_EPS) * lnw_ref[...] + lnb_ref[...]
    o_ref[0] = ((yn + bonus) * g).astype(o_ref.dtype)


def _rwkv_mixer(p_rw, mu, w0, wa, a0, g2, k_k, k_a, r_k, lnw, lnb, tt):
    b, t, _ = p_rw.shape
    gsum = jnp.asarray(np.kron(np.eye(MXU_DIM // RW_HEAD_DIM), np.ones((RW_HEAD_DIM, RW_HEAD_DIM))), BF16)
    tril = jnp.asarray(np.kron(np.eye(RW_CUM_CHUNKS), np.tril(np.ones((RW_CHUNK, RW_CHUNK)))), BF16)
    n_pairs = RW_HEADS // 2
    vec = lambda: _const_spec((1, RW_WIDTH))
    seq_scratch = lambda: pltpu.VMEM((tt, RW_WIDTH), F32)
    return pl.pallas_call(
        _rwkv_kernel,
        grid=(b, t // tt),
        in_specs=[pl.BlockSpec((1, tt, RW_COLS), lambda i, j: (i, j, 0)),
                  _const_spec((1, RW_COLS)), vec(), _const_spec(wa.shape), vec(),
                  _const_spec(g2.shape), vec(), vec(), vec(), vec(), vec(),
                  _const_spec(gsum.shape), _const_spec(tril.shape)],
        out_specs=pl.BlockSpec((1, tt, RW_WIDTH), lambda i, j: (i, j, 0)),
        out_shape=jax.ShapeDtypeStruct((b, t, RW_WIDTH), BF16),
        scratch_shapes=[seq_scratch() for _ in range(7)]
        + [pltpu.VMEM((n_pairs, LANES, LANES), F32),
           pltpu.VMEM((RW_COLS // LANES, SUBLANES + tt, LANES), F32)],
        compiler_params=pltpu.CompilerParams(dimension_semantics=("arbitrary", "arbitrary"),
                                             vmem_limit_bytes=VMEM_LIMIT),
        name="rwkv7_mixer",
    )(p_rw, mu, w0, wa, a0, g2, k_k, k_a, r_k, lnw, lnb, gsum, tril)


def _proj_retention_kernel(tiles_per_row, x_ref, g_ref, wrw_ref, wret_ref, wg_ref, pos_ref, ones_ref,
                           prw_ref, pg_ref, o_ref, p_s, vg_s, q_s, k_s, o_s, dmask_s, state_s):
    tt = x_ref.shape[0]
    C = RET_CHUNK
    QW = RET_QK_WIDTH
    half = RET_QK_DIM // 2
    n_pairs = RET_HEADS // 2
    step = pl.program_id(0)

    @pl.when(step == 0)
    def _():
        p_s[...] = jnp.zeros_like(p_s)

    @pl.when((lax.rem(step, tiles_per_row) == 1 % tiles_per_row) | (step == 0))
    def _():
        state_s[...] = jnp.zeros_like(state_s)

    hn = _rms_norm(x_ref[...], g_ref[...]).astype(BF16)
    vg_s[...] = p_s[:, 2 * QW:]

    lane = lax.broadcasted_iota(jnp.int32, (1, LANES), 1)
    freq = (lane & (half - 1)).astype(F32)
    inv = jnp.exp(freq * (-math.log(ROPE_BASE) / half))
    first = (lane & (RET_QK_DIM - 1)) < half
    n_copies = LANES // half
    rp = tt // n_copies
    lane_blk = lane >> int(math.log2(half))
    pos = pos_ref[...].astype(F32)
    ang = pos[0:rp, :] * inv
    for g in range(1, n_copies):
        ang = jnp.where(lane_blk == g, pos[g * rp:(g + 1) * rp, :] * inv, ang)

    def spread(tab):
        rolled = [tab] + [pltpu.roll(tab, half * k, 1) for k in range(1, n_copies)]
        groups = []
        for g in range(n_copies):
            full = rolled[(0 - g) % n_copies]
            for blk in range(1, n_copies):
                full = jnp.where(lane_blk == blk, rolled[(blk - g) % n_copies], full)
            groups.append(full)
        return jnp.concatenate(groups, axis=0)

    cos = spread(jnp.cos(ang))
    sin = spread(jnp.sin(ang))
    sin = jnp.where(first, -sin, sin)

    def rope(z):
        swapped = jnp.where(first, pltpu.roll(z, LANES - half, 1), pltpu.roll(z, half, 1))
        return z * cos + swapped * sin

    for j in range(n_pairs):
        sl = slice(LANES * j, LANES * (j + 1))
        q_s[:, sl] = rope(p_s[:, sl].astype(F32))
        k_s[:, sl] = rope(p_s[:, QW + LANES * j:QW + LANES * (j + 1)].astype(F32)) * (RET_QK_DIM ** -0.5)

    prw_ref[...] = jnp.dot(hn, wrw_ref[...], preferred_element_type=F32).astype(BF16)

    lane2 = lax.broadcasted_iota(jnp.int32, (C, 2 * LANES), 1)
    trow2 = lax.broadcasted_iota(jnp.int32, (C, 2 * LANES), 0)
    lane1 = lax.broadcasted_iota(jnp.int32, (C, LANES), 1)
    trow1 = lax.broadcasted_iota(jnp.int32, (C, LANES), 0).astype(F32)
    m0 = lane1 < RET_QK_DIM
    head_of_lane = int(math.log2(LANES))
    head_of_qk = int(math.log2(RET_QK_DIM))
    q_decay, k_decay, chunk_decay = [], [], []
    for j in range(n_pairs):
        lg = lambda hsel: jnp.log1p(-jnp.exp2(-5.0 - (2 * j + hsel).astype(F32)))
        lg2 = lg(lane2 >> head_of_lane)
        rel = (trow2 - (lane2 & (LANES - 1))).astype(F32)
        dmask_s[j] = jnp.where(rel >= 0, jnp.exp(jnp.maximum(rel, 0.0) * lg2), 0.0)
        lg1 = lg(lane1 >> head_of_qk)
        q_decay.append(jnp.exp((trow1 + 1.0) * lg1))
        k_decay.append(jnp.exp((C - 1.0 - trow1) * lg1))
        srow = lax.broadcasted_iota(jnp.int32, (LANES, LANES), 0) >> head_of_qk
        chunk_decay.append(jnp.exp(C * lg(srow)))

    def bd(x):
        z = jnp.zeros_like(x)
        return jnp.concatenate([jnp.where(m0, x, z), jnp.where(m0, z, x)], axis=0)

    units = [(c, j) for c in range(tt // C) for j in range(n_pairs)]
    hrow = lax.broadcasted_iota(jnp.int32, (LANES, LANES), 0) < RET_QK_DIM
    vcols = lambda j: slice(2 * LANES * j, 2 * LANES * (j + 1))
    scores, incr, qd = {}, {}, {}
    for c, j in units:
        rows = slice(c * C, (c + 1) * C)
        sl = slice(LANES * j, LANES * (j + 1))
        qj = q_s[rows, sl]
        kj = k_s[rows, sl]
        scores[c, j] = (_dot_nt(qj, bd(kj.astype(BF16))) * dmask_s[j]).astype(BF16)
        qd[c, j] = (qj * q_decay[j]).astype(BF16)
        x = _dot_tn(kj * k_decay[j], vg_s[rows, vcols(j)])
        incr[c, j] = jnp.where(hrow, x[:, 0:LANES], x[:, LANES:2 * LANES])
    state = {}
    for j in range(n_pairs):
        r_cur = state_s[j]
        for c in range(tt // C):
            state[c, j] = r_cur.astype(BF16)
            r_cur = r_cur * chunk_decay[j] + incr[c, j]
        state_s[j] = r_cur
    pg_ref[...] = jnp.dot(hn, wg_ref[...], preferred_element_type=F32).astype(BF16)
    for c, j in units:
        rows = slice(c * C, (c + 1) * C)
        vpair = vg_s[rows, vcols(j)]
        zq = jnp.zeros_like(qd[c, j])
        for hh in range(2):
            h = 2 * j + hh
            qm = jnp.where(m0, qd[c, j], zq) if hh == 0 else jnp.where(m0, zq, qd[c, j])
            lhs = jnp.concatenate([scores[c, j][:, LANES * hh:LANES * (hh + 1)], qm], axis=1)
            rhs = jnp.concatenate([vpair[:, LANES * hh:LANES * (hh + 1)], state[c, j]], axis=0)
            o_s[rows, LANES * h:LANES * (h + 1)] = jnp.dot(lhs, rhs, preferred_element_type=F32)

    def project_ret(piece):
        cols = slice(piece * RET_COLS // 3, (piece + 1) * RET_COLS // 3)
        p_s[:, cols] = jnp.dot(hn, wret_ref[:, cols], preferred_element_type=F32).astype(BF16)

    ones = ones_ref[...]
    inv_n = 1.0 / RET_V_DIM
    blocks = [slice(c0, c0 + MXU_DIM) for c0 in range(0, RET_V_WIDTH, MXU_DIM)]
    project_ret(0)
    means = [_group_sum(o_s[:, sl], ones) * inv_n for sl in blocks]
    project_ret(1)
    centred = [o_s[:, sl] - m for sl, m in zip(blocks, means)]
    variances = [_group_sum(oc * oc, ones) * inv_n for oc in centred]
    project_ret(2)
    for sl, oc, var in zip(blocks, centred, variances):
        gate = vg_s[:, RET_V_WIDTH + sl.start:RET_V_WIDTH + sl.stop].astype(F32)
        o_ref[:, sl] = (gate * jax.nn.sigmoid(gate) * oc * lax.rsqrt(var + RET_GN_EPS)).astype(o_ref.dtype)


def _proj_retention(x2, g, w_rw, w_ret, w_gate, pos2, tt, tiles_per_row):
    n = x2.shape[0]
    n_tiles = n // tt
    ones = jnp.asarray(np.kron(np.eye(MXU_DIM // RET_V_DIM), np.ones((RET_V_DIM, RET_V_DIM))), BF16)
    n_pairs = RET_HEADS // 2
    cur = lambda w: pl.BlockSpec((tt, w), lambda i: (jnp.minimum(i, n_tiles - 1), 0))
    prev = lambda w: pl.BlockSpec((tt, w), lambda i: (jnp.maximum(i - 1, 0), 0))
    return pl.pallas_call(
        functools.partial(_proj_retention_kernel, tiles_per_row),
        grid=(n_tiles + 1,),
        in_specs=[cur(D_MODEL), _const_spec((1, D_MODEL)), _const_spec(w_rw.shape),
                  _const_spec(w_ret.shape), _const_spec(w_gate.shape), prev(1), _const_spec(ones.shape)],
        out_specs=[cur(RW_COLS), cur(2 * D_MODEL), prev(RET_V_WIDTH)],
        out_shape=[jax.ShapeDtypeStruct((n, RW_COLS), BF16),
                   jax.ShapeDtypeStruct((n, 2 * D_MODEL), BF16),
                   jax.ShapeDtypeStruct((n, RET_V_WIDTH), BF16)],
        scratch_shapes=[pltpu.VMEM((tt, RET_COLS), BF16), pltpu.VMEM((tt, 2 * RET_V_WIDTH), BF16),
                        pltpu.VMEM((tt, RET_QK_WIDTH), F32), pltpu.VMEM((tt, RET_QK_WIDTH), F32),
                        pltpu.VMEM((tt, RET_V_WIDTH), F32),
                        pltpu.VMEM((n_pairs, RET_CHUNK, 2 * LANES), F32),
                        pltpu.VMEM((n_pairs, LANES, LANES), F32)],
        compiler_params=pltpu.CompilerParams(dimension_semantics=("arbitrary",),
                                             vmem_limit_bytes=VMEM_LIMIT),
        name="projection_retention",
    )(x2, g, w_rw, w_ret, w_gate, pos2, ones)


def _merge_kernel(x_ref, yrw_ref, yret_ref, pg_ref, wbrw_ref, wbret_ref, wout_ref,
                  npost_ref, nffn_ref, h_ref, hn_ref):
    tm = x_ref.shape[0]
    subs = [slice(r, r + tm // MERGE_SPLIT) for r in range(0, tm, tm // MERGE_SPLIT)]
    branches = [(jnp.dot(yrw_ref[rs, :], wbrw_ref[...], preferred_element_type=F32),
                 jnp.dot(yret_ref[rs, :], wbret_ref[...], preferred_element_type=F32)) for rs in subs]

    def finish(rs, mx):
        h = x_ref[rs, :] + _rms_norm(mx, npost_ref[...])
        h_ref[rs, :] = h
        hn_ref[rs, :] = _rms_norm(h, nffn_ref[...]).astype(BF16)

    pending = None
    for rs, (b_rw, b_ret) in zip(subs, branches):
        g_rw = jax.nn.sigmoid(pg_ref[rs, 0:D_MODEL].astype(F32))
        g_ret = jax.nn.sigmoid(pg_ref[rs, D_MODEL:2 * D_MODEL].astype(F32))
        mx = _dot(g_rw * b_rw + g_ret * b_ret, wout_ref[...])
        if pending is not None:
            finish(*pending)
        pending = (rs, mx)
    finish(*pending)


def _merge(x2, y_rw, y_ret, p_gate, wb_rw, wb_ret, w_out, n_post, n_ffn, tm):
    n = x2.shape[0]
    row = lambda w: pl.BlockSpec((tm, w), lambda i: (i, 0))
    return pl.pallas_call(
        _merge_kernel,
        grid=(n // tm,),
        in_specs=[row(D_MODEL), row(RW_WIDTH), row(RET_V_WIDTH), row(2 * D_MODEL),
                  _const_spec(wb_rw.shape), _const_spec(wb_ret.shape), _const_spec(w_out.shape),
                  _const_spec((1, D_MODEL)), _const_spec((1, D_MODEL))],
        out_specs=[row(D_MODEL), row(D_MODEL)],
        out_shape=[jax.ShapeDtypeStruct((n, D_MODEL), F32),
                   jax.ShapeDtypeStruct((n, D_MODEL), BF16)],
        compiler_params=pltpu.CompilerParams(dimension_semantics=("arbitrary",),
                                             vmem_limit_bytes=VMEM_LIMIT),
        name="merge_out_projection",
    )(x2, y_rw, y_ret, p_gate, wb_rw, wb_ret, w_out, n_post, n_ffn)


def _gelu_tanh(x):
    k0 = -2.0 * math.sqrt(2.0 / math.pi) * math.log2(math.e)
    k1 = k0 * 0.044715
    e = jnp.exp2(x * (k1 * (x * x) + k0))
    return x * (1.0 / (1.0 + e))


def _ffn_kernel(hn_ref, h_ref, wup_ref, cw_ref, cb_ref, wdn_ref, npost_ref, o_ref, slab_s, acc_s):
    tt = hn_ref.shape[1]
    fc = FF_CHUNK
    first_tile = pl.program_id(1) == 0

    hn = hn_ref[0]

    def conv(u, col):
        outs = []
        for blk in range(fc // LANES):
            c0 = col + blk * LANES
            slab = slab_s.at[c0 // LANES]
            prev = jnp.where(first_tile, 0.0, slab[tt:tt + SUBLANES, :])
            slab[0:SUBLANES, :] = prev
            ub = u[:, blk * LANES:(blk + 1) * LANES]
            slab[SUBLANES:SUBLANES + tt, :] = ub
            s1 = slab[SUBLANES - 1:SUBLANES - 1 + tt, :]
            s2 = slab[SUBLANES - 2:SUBLANES - 2 + tt, :]
            cw = cw_ref[:, c0:c0 + LANES]
            outs.append(ub * cw[2:3, :] + s1 * cw[1:2, :] + s2 * cw[0:1, :] + cb_ref[:, c0:c0 + LANES])
        return jnp.concatenate(outs, axis=1)

    def up(j):
        cg = j * fc
        cv = D_FF + j * fc
        return (jnp.dot(hn, wup_ref[:, cg:cg + fc], preferred_element_type=F32),
                jnp.dot(hn, wup_ref[:, cv:cv + fc], preferred_element_type=F32))

    n_chunks = D_FF // fc
    u_next = up(0)
    acts = []
    for j in range(n_chunks):
        cg = j * fc
        cv = D_FF + j * fc
        u_gate, u_val = u_next
        if j + 1 < n_chunks:
            u_next = up(j + 1)
        gate = conv(u_gate, cg)
        val = conv(u_val, cv)
        acts.append((_gelu_tanh(gate) * val).astype(BF16))
        if len(acts) == FF_DOWN_GROUP or j == n_chunks - 1:
            k0 = (j + 1 - len(acts)) * fc
            act = acts[0] if len(acts) == 1 else jnp.concatenate(acts, axis=1)
            part = jnp.dot(act, wdn_ref[k0:k0 + len(acts) * fc, :], preferred_element_type=F32)
            if k0 == 0:
                acc_s[...] = part
            else:
                acc_s[...] += part
            acts = []

    o_ref[0] = h_ref[0] + _rms_norm(acc_s[...], npost_ref[...])


def _conv_ffn(hn3, h3, w_up, conv_w, conv_b, w_down, n_post, tt):
    b, t, _ = hn3.shape
    tile = lambda: pl.BlockSpec((1, tt, D_MODEL), lambda i, j: (i, j, 0))
    return pl.pallas_call(
        _ffn_kernel,
        grid=(b, t // tt),
        in_specs=[tile(), tile(), _const_spec(w_up.shape), _const_spec(conv_w.shape),
                  _const_spec(conv_b.shape), _const_spec(w_down.shape), _const_spec((1, D_MODEL))],
        out_specs=tile(),
        out_shape=jax.ShapeDtypeStruct((b, t, D_MODEL), F32),
        scratch_shapes=[pltpu.VMEM((2 * D_FF // LANES, SUBLANES + tt, LANES), F32),
                        pltpu.VMEM((tt, D_MODEL), F32)],
        compiler_params=pltpu.CompilerParams(dimension_semantics=("arbitrary", "arbitrary"),
                                             vmem_limit_bytes=VMEM_LIMIT),
        name="conv_ffn",
    )(hn3, h3, w_up, conv_w, conv_b, w_down, n_post)


def kernel(x, positions, norm_mix_pre, norm_mix_post, norm_ffn_pre, norm_ffn_post, w_in, rw_mu, rw_w0, rw_w2, rw_a0, rw_a2, rw_g2, rw_k_k, rw_k_a, rw_r_k, rw_lnx_w, rw_lnx_b, w_branch_rw, w_branch_ret, w_out, ffn_w_up, ffn_conv_w, ffn_conv_b, ffn_w_down):
    b, t, d = x.shape
    assert d == D_MODEL and norm_mix_pre.shape[0] == 1
    n = b * t
    tm = min(TOKEN_TILE, n)
    tt = min(SEQ_TILE, t)
    assert n % tm == 0 and t % tt == 0 and tt % RET_CHUNK == 0
    vec = lambda a: a[0].reshape(1, -1).astype(F32)

    w_in_b = w_in[0].astype(BF16)
    w_rw = w_in_b[:, 0:RW_COLS]
    w_ret = w_in_b[:, RW_COLS:RW_COLS + RET_COLS]
    w_gate = w_in_b[:, RW_COLS + RET_COLS:]
    zeros = jnp.zeros((RW_DECAY_RANK, RW_WIDTH), F32)
    wa = jnp.concatenate([jnp.concatenate([rw_w2[0], zeros], axis=1),
                          jnp.concatenate([zeros, rw_a2[0]], axis=1)], axis=0).astype(BF16)

    x2 = x.reshape(n, d)
    p_rw, p_gate, y_ret = _proj_retention(x2, vec(norm_mix_pre), w_rw, w_ret, w_gate,
                                          positions.reshape(n, 1), tt, t // tt)

    y_rw = _rwkv_mixer(p_rw.reshape(b, t, RW_COLS), vec(rw_mu), vec(rw_w0), wa, vec(rw_a0),
                       rw_g2[0].astype(BF16), vec(rw_k_k), vec(rw_k_a), vec(rw_r_k),
                       vec(rw_lnx_w), vec(rw_lnx_b), tt)

    h, hn = _merge(x2, y_rw.reshape(n, RW_WIDTH), y_ret, p_gate,
                   w_branch_rw[0].astype(BF16), w_branch_ret[0].astype(BF16), w_out[0].astype(BF16),
                   vec(norm_mix_post), vec(norm_ffn_pre), tm)

    out = _conv_ffn(hn.reshape(b, t, d), h.reshape(b, t, d), ffn_w_up[0].astype(BF16),
                    ffn_conv_w[0].astype(F32), vec(ffn_conv_b), ffn_w_down[0].astype(BF16),
                    vec(norm_ffn_post), tt)
    return out.astype(x.dtype)
```

```python
import functools
import math

import numpy as np
import jax
import jax.numpy as jnp
from jax import lax
from jax.experimental import pallas as pl
from jax.experimental.pallas import tpu as pltpu

F32 = jnp.float32
BF16 = jnp.bfloat16

D_MODEL = 1024
RW_HEAD_DIM = 64
RW_WIDTH = 512
RW_HEADS = 8
RW_DECAY_RANK = 64
RW_ICLR_RANK = 64
RW_GATE_RANK = 128
RW_COLS = 3 * RW_WIDTH + RW_DECAY_RANK + RW_ICLR_RANK + RW_GATE_RANK
RW_GN_EPS = 64e-5
RW_CHUNK = 64
RW_GROUP = 16
RW_CUM_CHUNKS = 2
RET_QK_DIM = 64
RET_QK_WIDTH = 512
RET_HEADS = 8
RET_V_DIM = 128
RET_V_WIDTH = 1024
RET_COLS = 2 * RET_QK_WIDTH + 2 * RET_V_WIDTH
RET_CHUNK = 128
ROPE_BASE = 10000.0
RET_GN_EPS = 1e-5
D_FF = 2816
CONV_WIDTH = 3
RMS_EPS = 1e-6

LANES = 128
SUBLANES = 8
MXU_DIM = 256
VMEM_LIMIT = 56 * 1024 * 1024
TOKEN_TILE = 1024
SEQ_TILE = 512
RW_SEQ_TILE = 1024
MERGE_SPLIT = 8
FF_CHUNK = 256
FF_DOWN_GROUP = 11


def _dot(a, b):
    return jnp.dot(a.astype(BF16), b.astype(BF16), preferred_element_type=F32)


def _dot_nt(a, b):
    return lax.dot_general(a.astype(BF16), b.astype(BF16), (((1,), (1,)), ((), ())),
                           preferred_element_type=F32)


def _dot_tn(a, b):
    return lax.dot_general(a.astype(BF16), b.astype(BF16), (((0,), (0,)), ((), ())),
                           preferred_element_type=F32)


def _group_sum(x, m):
    xb = x.astype(BF16)
    bw = m.shape[0]
    parts = [jnp.dot(xb[:, c:c + bw], m, preferred_element_type=F32) for c in range(0, x.shape[1], bw)]
    return parts[0] if len(parts) == 1 else jnp.concatenate(parts, axis=1)


def _rms_norm(x, g):
    ms = jnp.mean(x * x, axis=-1, keepdims=True)
    return x * lax.rsqrt(ms + RMS_EPS) * g


def _const_spec(shape):
    nd = len(shape)
    return pl.BlockSpec(shape, lambda *_: (0,) * nd, pipeline_mode=pl.Buffered(1))


def _rwkv_kernel(p_ref, mu_ref, w0_ref, wa_ref, a0_ref, g2_ref, kk_ref, ka_ref, rk_ref,
                 lnw_ref, lnb_ref, gsum_ref, tril_ref, o_ref,
                 r_s, k_s, v_s, a_s, b_s, lw_s, y_s, state_s, slab_s):
    tt = p_ref.shape[1]
    W = RW_WIDTH
    C = RW_CHUNK

    @pl.when(pl.program_id(1) == 0)
    def _():
        state_s[...] = jnp.zeros_like(state_s)

    first_tile = pl.program_id(1) == 0
    blocks = []
    for blk in range(RW_COLS // LANES):
        sl = slice(blk * LANES, (blk + 1) * LANES)
        slab = slab_s.at[blk]
        slab[0:SUBLANES, :] = jnp.where(first_tile, 0.0, slab[tt:tt + SUBLANES, :])
        pb = p_ref[0, :, sl].astype(F32)
        slab[SUBLANES:SUBLANES + tt, :] = pb
        blocks.append(pb + (slab[SUBLANES - 1:SUBLANES - 1 + tt, :] - pb) * mu_ref[:, sl])
    p = jnp.concatenate(blocks, axis=1)

    r = p[:, 0:W]
    k = p[:, W:2 * W]
    v = p[:, 2 * W:3 * W]
    lane = lax.broadcasted_iota(jnp.int32, (1, LANES), 1)
    wa_in = p[:, 3 * W:3 * W + LANES]
    wa_in = jnp.where(lane < RW_DECAY_RANK, jnp.tanh(wa_in), wa_in)
    wa = _dot(wa_in, wa_ref[...])
    gd = p[:, 3 * W + LANES:3 * W + 2 * LANES]

    lw_s[...] = -math.exp(-0.5) * jax.nn.sigmoid(w0_ref[...] + wa[:, 0:W])
    a_lr = jax.nn.sigmoid(a0_ref[...] + wa[:, W:2 * W])

    gsum = gsum_ref[...]
    kk = k * kk_ref[...]
    kk = kk * lax.rsqrt(jnp.maximum(_group_sum(kk * kk, gsum), 1e-24))
    k = k * (1.0 + (a_lr - 1.0) * ka_ref[...])

    r_s[...] = r
    k_s[...] = k
    v_s[...] = v
    a_s[...] = -kk
    b_s[...] = kk * a_lr

    lane2 = lax.broadcasted_iota(jnp.int32, (C, LANES), 1)
    trow = lax.broadcasted_iota(jnp.int32, (C, LANES), 0)
    m0 = lane2 < RW_HEAD_DIM
    scol = jnp.where(m0, lane2, lane2 - RW_HEAD_DIM)
    strict = trow > scol
    incl = trow >= scol
    brow = lax.broadcasted_iota(jnp.int32, (LANES, LANES), 0) < RW_HEAD_DIM
    bcol = lax.broadcasted_iota(jnp.int32, (LANES, LANES), 1) < RW_HEAD_DIM
    blockmask = brow == bcol
    tril = tril_ref[...]

    def bd(x):
        z = jnp.zeros_like(x)
        return jnp.concatenate([jnp.where(m0, x, z), jnp.where(m0, z, x)], axis=0)

    n_pairs = RW_HEADS // 2
    grp = min(RW_GROUP, tt // C)
    gc = grp * C
    grow = lax.broadcasted_iota(jnp.int32, (gc, 1), 0)
    n_levels = int(math.log2(C))

    def prep_body(i, carry):
        rows = pl.ds(pl.multiple_of(i * gc, gc), gc)
        lw = lw_s[rows, :]
        h1 = lw.astype(BF16)
        h2 = (lw - h1.astype(F32)).astype(BF16)
        cb = tril.shape[0]
        cum = jnp.concatenate(
            [jnp.dot(tril, h1[q * cb:(q + 1) * cb], preferred_element_type=F32)
             + jnp.dot(tril, h2[q * cb:(q + 1) * cb], preferred_element_type=F32)
             for q in range(gc // cb)], axis=0)
        tot = cum[C - 1:C, :]
        for g in range(1, grp):
            tot = jnp.where(grow >= g * C, cum[(g + 1) * C - 1:(g + 1) * C, :], tot)
        w_inv = jnp.exp(-cum)
        w_rem = jnp.exp(tot - cum)
        rc = r_s[rows, :]
        kc = k_s[rows, :]
        bc = b_s[rows, :]
        v32c = v_s[rows, :]
        vc = v32c.astype(BF16)
        rt32 = rc * jnp.exp(cum)
        at32 = a_s[rows, :] * jnp.exp(cum - lw)
        rt = rt32.astype(BF16)
        at = at32.astype(BF16)
        bt = (bc * w_inv).astype(BF16)
        kt = (kc * w_inv).astype(BF16)
        bh = (bc * w_rem).astype(BF16)
        kh = (kc * w_rem).astype(BF16)

        units = [(g, j) for g in range(grp) for j in range(n_pairs)]
        blk = lambda arr, g, j: arr[g * C:(g + 1) * C, LANES * j:LANES * (j + 1)]
        zero = jnp.zeros((C, LANES), F32)
        stack2 = lambda x: jnp.concatenate([x, x], axis=0)
        zero2 = jnp.zeros((2 * C, LANES), F32)
        a_rb, a_rk, z, n_bd = {}, {}, {}, {}
        ak_bd, vw, a_st = {}, {}, {}
        for u in units:
            lhs = jnp.concatenate([blk(at, *u), blk(rt, *u)], axis=0)
            gram = _dot_nt(lhs, jnp.concatenate([bd(blk(bt, *u)), bd(blk(kt, *u))], axis=0))
            a_ab = jnp.where(strict, gram[0:C, 0:LANES], zero)
            a_ak = jnp.where(strict, gram[0:C, LANES:2 * LANES], zero)
            n_bd[u] = jnp.where(blockmask, stack2(a_ab), zero2).astype(BF16)
            ak_bd[u] = jnp.where(blockmask, stack2(a_ak), zero2).astype(BF16)
            a_rb[u] = jnp.where(incl, gram[C:2 * C, 0:LANES], zero).astype(BF16)
            a_rk[u] = jnp.where(incl, gram[C:2 * C, LANES:2 * LANES], zero).astype(BF16)
            v32 = blk(v32c, *u)
            vw[u] = jnp.concatenate([pltpu.roll(v32, RW_HEAD_DIM, 1), v32], axis=0).astype(BF16)
            a32 = blk(at32, *u)
            a_st[u] = jnp.concatenate([a32, pltpu.roll(a32, RW_HEAD_DIM, 1)], axis=0)
        for u in units:
            akv = jnp.dot(ak_bd[u], vw[u], preferred_element_type=F32)
            z[u] = jnp.where(bcol, a_st[u], akv)
        for lvl in range(n_levels):
            last = lvl == n_levels - 1
            for u in units:
                zb = z[u].astype(BF16)
                rhs = zb if last else jnp.concatenate([zb, n_bd[u]], axis=1)
                comb = jnp.dot(n_bd[u], rhs, preferred_element_type=F32)
                z[u] = z[u] + comb[:, 0:LANES]
                if not last:
                    n_bd[u] = comb[:, LANES:2 * LANES].astype(BF16)

        def finish(u):
            top = z[u][0:C]
            bot = z[u][C:2 * C]
            apb = jnp.where(m0, top, pltpu.roll(bot, RW_HEAD_DIM, 1)).astype(BF16)
            uvb = jnp.where(m0, pltpu.roll(top, RW_HEAD_DIM, 1), bot).astype(BF16)
            out = jnp.dot(a_rb[u], jnp.concatenate([bd(apb), bd(uvb)], axis=1),
                          preferred_element_type=F32)
            rp = (blk(rt32, *u) + out[:, 0:LANES]).astype(BF16)
            yv = out[:, LANES:2 * LANES] + jnp.dot(a_rk[u], bd(blk(vc, *u)), preferred_element_type=F32)
            x = _dot_tn(apb, blk(bh, *u))
            x = jnp.where(blockmask, x, jnp.zeros_like(x)).astype(BF16)
            e = _dot_tn(jnp.concatenate([uvb, blk(vc, *u)], axis=0),
                        jnp.concatenate([blk(bh, *u), blk(kh, *u)], axis=0))
            e = jnp.where(blockmask, e, jnp.zeros_like(e))
            return rp, yv, x, e

        s_cur = [state_s[j] for j in range(n_pairs)]

        def scan_step(g, fin):
            crow = pl.ds(pl.multiple_of(i * gc + g * C, C), C)
            w_tot = jnp.exp(cum[(g + 1) * C - 1:(g + 1) * C, :])
            for j in range(n_pairs):
                sl = slice(LANES * j, LANES * (j + 1))
                rp, yv, x, e = fin[j]
                sb = s_cur[j].astype(BF16)
                y_s[crow, sl] = _dot_nt(rp, sb) + yv
                s_cur[j] = s_cur[j] * w_tot[:, sl] + e + jnp.dot(sb, x, preferred_element_type=F32)

        pending = None
        for g in range(grp):
            fin = [finish((g, j)) for j in range(n_pairs)]
            if pending is not None:
                scan_step(*pending)
            pending = (g, fin)
        scan_step(*pending)
        for j in range(n_pairs):
            state_s[j] = s_cur[j]
        return carry

    lax.fori_loop(0, tt // gc, prep_body, 0)

    g = _dot(jax.nn.sigmoid(gd), g2_ref[...])
    bonus = _group_sum(r_s[...] * k_s[...] * rk_ref[...], gsum) * v_s[...]
    y = y_s[...]
    inv_n = 1.0 / RW_HEAD_DIM
    mean = _group_sum(y, gsum) * inv_n
    yc = y - mean
    var = _group_sum(yc * yc, gsum) * inv_n
    yn = yc * lax.rsqrt(var + RW_GN_EPS) * lnw_ref[...] + lnb_ref[...]
    o_ref[0] = ((yn + bonus) * g).astype(o_ref.dtype)


def _rwkv_mixer(p_rw, mu, w0, wa, a0, g2, k_k, k_a, r_k, lnw, lnb, tt):
    b, t, _ = p_rw.shape
    gsum = jnp.asarray(np.kron(np.eye(MXU_DIM // RW_HEAD_DIM), np.ones((RW_HEAD_DIM, RW_HEAD_DIM))), BF16)
    tril = jnp.asarray(np.kron(np.eye(RW_CUM_CHUNKS), np.tril(np.ones((RW_CHUNK, RW_CHUNK)))), BF16)
    n_pairs = RW_HEADS // 2
    vec = lambda: _const_spec((1, RW_WIDTH))
    seq_scratch = lambda: pltpu.VMEM((tt, RW_WIDTH), F32)
    return pl.pallas_call(
        _rwkv_kernel,
        grid=(b, t // tt),
        in_specs=[pl.BlockSpec((1, tt, RW_COLS), lambda i, j: (i, j, 0)),
                  _const_spec((1, RW_COLS)), vec(), _const_spec(wa.shape), vec(),
                  _const_spec(g2.shape), vec(), vec(), vec(), vec(), vec(),
                  _const_spec(gsum.shape), _const_spec(tril.shape)],
        out_specs=pl.BlockSpec((1, tt, RW_WIDTH), lambda i, j: (i, j, 0)),
        out_shape=jax.ShapeDtypeStruct((b, t, RW_WIDTH), BF16),
        scratch_shapes=[seq_scratch() for _ in range(7)]
        + [pltpu.VMEM((n_pairs, LANES, LANES), F32),
           pltpu.VMEM((RW_COLS // LANES, SUBLANES + tt, LANES), F32)],
        compiler_params=pltpu.CompilerParams(dimension_semantics=("arbitrary", "arbitrary"),
                                             vmem_limit_bytes=VMEM_LIMIT),
        name="rwkv7_mixer",
    )(p_rw, mu, w0, wa, a0, g2, k_k, k_a, r_k, lnw, lnb, gsum, tril)


def _proj_retention_kernel(tiles_per_row, x_ref, g_ref, wrw_ref, wret_ref, wg_ref, pos_ref, ones_ref,
                           prw_ref, pg_ref, o_ref, p_s, vg_s, q_s, k_s, o_s, dmask_s, state_s):
    tt = x_ref.shape[0]
    C = RET_CHUNK
    QW = RET_QK_WIDTH
    half = RET_QK_DIM // 2
    n_pairs = RET_HEADS // 2
    step = pl.program_id(0)

    @pl.when(step == 0)
    def _():
        p_s[...] = jnp.zeros_like(p_s)

    @pl.when((lax.rem(step, tiles_per_row) == 1 % tiles_per_row) | (step == 0))
    def _():
        state_s[...] = jnp.zeros_like(state_s)

    hn = _rms_norm(x_ref[...], g_ref[...]).astype(BF16)
    vg_s[...] = p_s[:, 2 * QW:]

    lane = lax.broadcasted_iota(jnp.int32, (1, LANES), 1)
    freq = (lane & (half - 1)).astype(F32)
    inv = jnp.exp(freq * (-math.log(ROPE_BASE) / half))
    first = (lane & (RET_QK_DIM - 1)) < half
    n_copies = LANES // half
    rp = tt // n_copies
    lane_blk = lane >> int(math.log2(half))
    pos = pos_ref[...].astype(F32)
    ang = pos[0:rp, :] * inv
    for g in range(1, n_copies):
        ang = jnp.where(lane_blk == g, pos[g * rp:(g + 1) * rp, :] * inv, ang)

    def spread(tab):
        rolled = [tab] + [pltpu.roll(tab, half * k, 1) for k in range(1, n_copies)]
        groups = []
        for g in range(n_copies):
            full = rolled[(0 - g) % n_copies]
            for blk in range(1, n_copies):
                full = jnp.where(lane_blk == blk, rolled[(blk - g) % n_copies], full)
            groups.append(full)
        return jnp.concatenate(groups, axis=0)

    cos = spread(jnp.cos(ang))
    sin = spread(jnp.sin(ang))
    sin = jnp.where(first, -sin, sin)

    def rope(z):
        swapped = jnp.where(first, pltpu.roll(z, LANES - half, 1), pltpu.roll(z, half, 1))
        return z * cos + swapped * sin

    for j in range(n_pairs):
        sl = slice(LANES * j, LANES * (j + 1))
        q_s[:, sl] = rope(p_s[:, sl].astype(F32))
        k_s[:, sl] = rope(p_s[:, QW + LANES * j:QW + LANES * (j + 1)].astype(F32)) * (RET_QK_DIM ** -0.5)

    prw_ref[...] = jnp.dot(hn, wrw_ref[...], preferred_element_type=F32).astype(BF16)

    lane2 = lax.broadcasted_iota(jnp.int32, (C, 2 * LANES), 1)
    trow2 = lax.broadcasted_iota(jnp.int32, (C, 2 * LANES), 0)
    lane1 = lax.broadcasted_iota(jnp.int32, (C, LANES), 1)
    trow1 = lax.broadcasted_iota(jnp.int32, (C, LANES), 0).astype(F32)
    m0 = lane1 < RET_QK_DIM
    head_of_lane = int(math.log2(LANES))
    head_of_qk = int(math.log2(RET_QK_DIM))
    q_decay, k_decay, chunk_decay = [], [], []
    for j in range(n_pairs):
        lg = lambda hsel: jnp.log1p(-jnp.exp2(-5.0 - (2 * j + hsel).astype(F32)))
        lg2 = lg(lane2 >> head_of_lane)
        rel = (trow2 - (lane2 & (LANES - 1))).astype(F32)
        dmask_s[j] = jnp.where(rel >= 0, jnp.exp(jnp.maximum(rel, 0.0) * lg2), 0.0)
        lg1 = lg(lane1 >> head_of_qk)
        q_decay.append(jnp.exp((trow1 + 1.0) * lg1))
        k_decay.append(jnp.exp((C - 1.0 - trow1) * lg1))
        srow = lax.broadcasted_iota(jnp.int32, (LANES, LANES), 0) >> head_of_qk
        chunk_decay.append(jnp.exp(C * lg(srow)))

    def bd(x):
        z = jnp.zeros_like(x)
        return jnp.concatenate([jnp.where(m0, x, z), jnp.where(m0, z, x)], axis=0)

    units = [(c, j) for c in range(tt // C) for j in range(n_pairs)]
    hrow = lax.broadcasted_iota(jnp.int32, (LANES, LANES), 0) < RET_QK_DIM
    vcols = lambda j: slice(2 * LANES * j, 2 * LANES * (j + 1))
    scores, incr, qd = {}, {}, {}
    for c, j in units:
        rows = slice(c * C, (c + 1) * C)
        sl = slice(LANES * j, LANES * (j + 1))
        qj = q_s[rows, sl]
        kj = k_s[rows, sl]
        scores[c, j] = (_dot_nt(qj, bd(kj.astype(BF16))) * dmask_s[j]).astype(BF16)
        qd[c, j] = (qj * q_decay[j]).astype(BF16)
        x = _dot_tn(kj * k_decay[j], vg_s[rows, vcols(j)])
        incr[c, j] = jnp.where(hrow, x[:, 0:LANES], x[:, LANES:2 * LANES])
    state = {}
    for j in range(n_pairs):
        r_cur = state_s[j]
        for c in range(tt // C):
            state[c, j] = r_cur.astype(BF16)
            r_cur = r_cur * chunk_decay[j] + incr[c, j]
        state_s[j] = r_cur
    pg_ref[...] = jnp.dot(hn, wg_ref[...], preferred_element_type=F32).astype(BF16)
    for c, j in units:
        rows = slice(c * C, (c + 1) * C)
        vpair = vg_s[rows, vcols(j)]
        zq = jnp.zeros_like(qd[c, j])
        for hh in range(2):
            h = 2 * j + hh
            qm = jnp.where(m0, qd[c, j], zq) if hh == 0 else jnp.where(m0, zq, qd[c, j])
            lhs = jnp.concatenate([scores[c, j][:, LANES * hh:LANES * (hh + 1)], qm], axis=1)
            rhs = jnp.concatenate([vpair[:, LANES * hh:LANES * (hh + 1)], state[c, j]], axis=0)
            o_s[rows, LANES * h:LANES * (h + 1)] = jnp.dot(lhs, rhs, preferred_element_type=F32)

    def project_ret(piece):
        cols = slice(piece * RET_COLS // 3, (piece + 1) * RET_COLS // 3)
        p_s[:, cols] = jnp.dot(hn, wret_ref[:, cols], preferred_element_type=F32).astype(BF16)

    ones = ones_ref[...]
    inv_n = 1.0 / RET_V_DIM
    blocks = [slice(c0, c0 + MXU_DIM) for c0 in range(0, RET_V_WIDTH, MXU_DIM)]
    project_ret(0)
    means = [_group_sum(o_s[:, sl], ones) * inv_n for sl in blocks]
    project_ret(1)
    centred = [o_s[:, sl] - m for sl, m in zip(blocks, means)]
    variances = [_group_sum(oc * oc, ones) * inv_n for oc in centred]
    project_ret(2)
    for sl, oc, var in zip(blocks, centred, variances):
        gate = vg_s[:, RET_V_WIDTH + sl.start:RET_V_WIDTH + sl.stop].astype(F32)
        o_ref[:, sl] = (gate * jax.nn.sigmoid(gate) * oc * lax.rsqrt(var + RET_GN_EPS)).astype(o_ref.dtype)


def _proj_retention(x2, g, w_rw, w_ret, w_gate, pos2, tt, tiles_per_row):
    n = x2.shape[0]
    n_tiles = n // tt
    ones = jnp.asarray(np.kron(np.eye(MXU_DIM // RET_V_DIM), np.ones((RET_V_DIM, RET_V_DIM))), BF16)
    n_pairs = RET_HEADS // 2
    cur = lambda w: pl.BlockSpec((tt, w), lambda i: (jnp.minimum(i, n_tiles - 1), 0))
    prev = lambda w: pl.BlockSpec((tt, w), lambda i: (jnp.maximum(i - 1, 0), 0))
    return pl.pallas_call(
        functools.partial(_proj_retention_kernel, tiles_per_row),
        grid=(n_tiles + 1,),
        in_specs=[cur(D_MODEL), _const_spec((1, D_MODEL)), _const_spec(w_rw.shape),
                  _const_spec(w_ret.shape), _const_spec(w_gate.shape), prev(1), _const_spec(ones.shape)],
        out_specs=[cur(RW_COLS), cur(2 * D_MODEL), prev(RET_V_WIDTH)],
        out_shape=[jax.ShapeDtypeStruct((n, RW_COLS), BF16),
                   jax.ShapeDtypeStruct((n, 2 * D_MODEL), BF16),
                   jax.ShapeDtypeStruct((n, RET_V_WIDTH), BF16)],
        scratch_shapes=[pltpu.VMEM((tt, RET_COLS), BF16), pltpu.VMEM((tt, 2 * RET_V_WIDTH), BF16),
                        pltpu.VMEM((tt, RET_QK_WIDTH), F32), pltpu.VMEM((tt, RET_QK_WIDTH), F32),
                        pltpu.VMEM((tt, RET_V_WIDTH), F32),
                        pltpu.VMEM((n_pairs, RET_CHUNK, 2 * LANES), F32),
                        pltpu.VMEM((n_pairs, LANES, LANES), F32)],
        compiler_params=pltpu.CompilerParams(dimension_semantics=("arbitrary",),
                                             vmem_limit_bytes=VMEM_LIMIT),
        name="projection_retention",
    )(x2, g, w_rw, w_ret, w_gate, pos2, ones)


def _merge_kernel(x_ref, yrw_ref, yret_ref, pg_ref, wbrw_ref, wbret_ref, wout_ref,
                  npost_ref, nffn_ref, h_ref, hn_ref):
    tm = x_ref.shape[0]
    subs = [slice(r, r + tm // MERGE_SPLIT) for r in range(0, tm, tm // MERGE_SPLIT)]
    branches = [(jnp.dot(yrw_ref[rs, :], wbrw_ref[...], preferred_element_type=F32),
                 jnp.dot(yret_ref[rs, :], wbret_ref[...], preferred_element_type=F32)) for rs in subs]

    def finish(rs, mx):
        h = x_ref[rs, :] + _rms_norm(mx, npost_ref[...])
        h_ref[rs, :] = h
        hn_ref[rs, :] = _rms_norm(h, nffn_ref[...]).astype(BF16)

    pending = None
    for rs, (b_rw, b_ret) in zip(subs, branches):
        g_rw = jax.nn.sigmoid(pg_ref[rs, 0:D_MODEL].astype(F32))
        g_ret = jax.nn.sigmoid(pg_ref[rs, D_MODEL:2 * D_MODEL].astype(F32))
        mx = _dot(g_rw * b_rw + g_ret * b_ret, wout_ref[...])
        if pending is not None:
            finish(*pending)
        pending = (rs, mx)
    finish(*pending)


def _merge(x2, y_rw, y_ret, p_gate, wb_rw, wb_ret, w_out, n_post, n_ffn, tm):
    n = x2.shape[0]
    row = lambda w: pl.BlockSpec((tm, w), lambda i: (i, 0))
    return pl.pallas_call(
        _merge_kernel,
        grid=(n // tm,),
        in_specs=[row(D_MODEL), row(RW_WIDTH), row(RET_V_WIDTH), row(2 * D_MODEL),
                  _const_spec(wb_rw.shape), _const_spec(wb_ret.shape), _const_spec(w_out.shape),
                  _const_spec((1, D_MODEL)), _const_spec((1, D_MODEL))],
        out_specs=[row(D_MODEL), row(D_MODEL)],
        out_shape=[jax.ShapeDtypeStruct((n, D_MODEL), F32),
                   jax.ShapeDtypeStruct((n, D_MODEL), BF16)],
        compiler_params=pltpu.CompilerParams(dimension_semantics=("arbitrary",),
                                             vmem_limit_bytes=VMEM_LIMIT),
        name="merge_out_projection",
    )(x2, y_rw, y_ret, p_gate, wb_rw, wb_ret, w_out, n_post, n_ffn)


def _gelu_tanh(x):
    k0 = -2.0 * math.sqrt(2.0 / math.pi) * math.log2(math.e)
    k1 = k0 * 0.044715
    e = jnp.exp2(x * (k1 * (x * x) + k0))
    return x * (1.0 / (1.0 + e))


def _ffn_kernel(hn_ref, h_ref, wup_ref, cw_ref, cb_ref, wdn_ref, npost_ref, o_ref, slab_s, acc_s):
    tt = hn_ref.shape[1]
    fc = FF_CHUNK
    first_tile = pl.program_id(1) == 0

    hn = hn_ref[0]

    def conv(u, col):
        outs = []
        for blk in range(fc // LANES):
            c0 = col + blk * LANES
            slab = slab_s.at[c0 // LANES]
            prev = jnp.where(first_tile, 0.0, slab[tt:tt + SUBLANES, :])
            slab[0:SUBLANES, :] = prev
            ub = u[:, blk * LANES:(blk + 1) * LANES]
            slab[SUBLANES:SUBLANES + tt, :] = ub
            s1 = slab[SUBLANES - 1:SUBLANES - 1 + tt, :]
            s2 = slab[SUBLANES - 2:SUBLANES - 2 + tt, :]
            cw = cw_ref[:, c0:c0 + LANES]
            outs.append(ub * cw[2:3, :] + s1 * cw[1:2, :] + s2 * cw[0:1, :] + cb_ref[:, c0:c0 + LANES])
        return jnp.concatenate(outs, axis=1)

    def up(j):
        cg = j * fc
        cv = D_FF + j * fc
        return (jnp.dot(hn, wup_ref[:, cg:cg + fc], preferred_element_type=F32),
                jnp.dot(hn, wup_ref[:, cv:cv + fc], preferred_element_type=F32))

    n_chunks = D_FF // fc
    u_next = up(0)
    acts = []
    for j in range(n_chunks):
        cg = j * fc
        cv = D_FF + j * fc
        u_gate, u_val = u_next
        if j + 1 < n_chunks:
            u_next = up(j + 1)
        gate = conv(u_gate, cg)
        val = conv(u_val, cv)
        acts.append((_gelu_tanh(gate) * val).astype(BF16))
        if len(acts) == FF_DOWN_GROUP or j == n_chunks - 1:
            k0 = (j + 1 - len(acts)) * fc
            act = acts[0] if len(acts) == 1 else jnp.concatenate(acts, axis=1)
            part = jnp.dot(act, wdn_ref[k0:k0 + len(acts) * fc, :], preferred_element_type=F32)
            if k0 == 0:
                acc_s[...] = part
            else:
                acc_s[...] += part
            acts = []

    o_ref[0] = h_ref[0] + _rms_norm(acc_s[...], npost_ref[...])


def _conv_ffn(hn3, h3, w_up, conv_w, conv_b, w_down, n_post, tt):
    b, t, _ = hn3.shape
    tile = lambda: pl.BlockSpec((1, tt, D_MODEL), lambda i, j: (i, j, 0))
    return pl.pallas_call(
        _ffn_kernel,
        grid=(b, t // tt),
        in_specs=[tile(), tile(), _const_spec(w_up.shape), _const_spec(conv_w.shape),
                  _const_spec(conv_b.shape), _const_spec(w_down.shape), _const_spec((1, D_MODEL))],
        out_specs=tile(),
        out_shape=jax.ShapeDtypeStruct((b, t, D_MODEL), F32),
        scratch_shapes=[pltpu.VMEM((2 * D_FF // LANES, SUBLANES + tt, LANES), F32),
                        pltpu.VMEM((tt, D_MODEL), F32)],
        compiler_params=pltpu.CompilerParams(dimension_semantics=("arbitrary", "arbitrary"),
                                             vmem_limit_bytes=VMEM_LIMIT),
        name="conv_ffn",
    )(hn3, h3, w_up, conv_w, conv_b, w_down, n_post)


def kernel(x, positions, norm_mix_pre, norm_mix_post, norm_ffn_pre, norm_ffn_post, w_in, rw_mu, rw_w0, rw_w2, rw_a0, rw_a2, rw_g2, rw_k_k, rw_k_a, rw_r_k, rw_lnx_w, rw_lnx_b, w_branch_rw, w_branch_ret, w_out, ffn_w_up, ffn_conv_w, ffn_conv_b, ffn_w_down):
    b, t, d = x.shape
    assert d == D_MODEL and norm_mix_pre.shape[0] == 1
    n = b * t
    tm = min(TOKEN_TILE, n)
    tt = min(SEQ_TILE, t)
    assert n % tm == 0 and t % tt == 0 and tt % RET_CHUNK == 0 and t % min(RW_SEQ_TILE, t) == 0
    assert ffn_conv_w.shape[1] == CONV_WIDTH == 3
    vec = lambda a: a[0].reshape(1, -1).astype(F32)

    w_in_b = w_in[0].astype(BF16)
    w_rw = w_in_b[:, 0:RW_COLS]
    w_ret = w_in_b[:, RW_COLS:RW_COLS + RET_COLS]
    w_gate = w_in_b[:, RW_COLS + RET_COLS:]
    zeros = jnp.zeros((RW_DECAY_RANK, RW_WIDTH), F32)
    wa = jnp.concatenate([jnp.concatenate([rw_w2[0], zeros], axis=1),
                          jnp.concatenate([zeros, rw_a2[0]], axis=1)], axis=0).astype(BF16)

    x2 = x.reshape(n, d)
    p_rw, p_gate, y_ret = _proj_retention(x2, vec(norm_mix_pre), w_rw, w_ret, w_gate,
                                          positions.reshape(n, 1), tt, t // tt)

    y_rw = _rwkv_mixer(p_rw.reshape(b, t, RW_COLS), vec(rw_mu), vec(rw_w0), wa, vec(rw_a0),
                       rw_g2[0].astype(BF16), vec(rw_k_k), vec(rw_k_a), vec(rw_r_k),
                       vec(rw_lnx_w), vec(rw_lnx_b), min(RW_SEQ_TILE, t))

    h, hn = _merge(x2, y_rw.reshape(n, RW_WIDTH), y_ret, p_gate,
                   w_branch_rw[0].astype(BF16), w_branch_ret[0].astype(BF16), w_out[0].astype(BF16),
                   vec(norm_mix_post), vec(norm_ffn_pre), tm)

    out = _conv_ffn(hn.reshape(b, t, d), h.reshape(b, t, d), ffn_w_up[0].astype(BF16),
                    ffn_conv_w[0].astype(F32), vec(ffn_conv_b), ffn_w_down[0].astype(BF16),
                    vec(norm_ffn_post), tt)
    return out.astype(x.dtype)
```

```python
import functools
import math

import numpy as np
import jax
import jax.numpy as jnp
from jax import lax
from jax.experimental import pallas as pl
from jax.experimental.pallas import tpu as pltpu

F32 = jnp.float32
BF16 = jnp.bfloat16

D_MODEL = 1024
RW_HEAD_DIM = 64
RW_WIDTH = 512
RW_HEADS = 8
RW_DECAY_RANK = 64
RW_ICLR_RANK = 64
RW_GATE_RANK = 128
RW_COLS = 3 * RW_WIDTH + RW_DECAY_RANK + RW_ICLR_RANK + RW_GATE_RANK
RW_GN_EPS = 64e-5
RW_CHUNK = 64
RW_GROUP = 16
RW_CUM_CHUNKS = 2
RET_QK_DIM = 64
RET_QK_WIDTH = 512
RET_HEADS = 8
RET_V_DIM = 128
RET_V_WIDTH = 1024
RET_COLS = 2 * RET_QK_WIDTH + 2 * RET_V_WIDTH
RET_CHUNK = 128
ROPE_BASE = 10000.0
RET_GN_EPS = 1e-5
D_FF = 2816
CONV_WIDTH = 3
RMS_EPS = 1e-6

LANES = 128
SUBLANES = 8
MXU_DIM = 256
VMEM_LIMIT = 56 * 1024 * 1024
TOKEN_TILE = 1024
SEQ_TILE = 512
RW_SEQ_TILE = 1024
MERGE_SPLIT = 8
FF_CHUNK = 256
FF_DOWN_GROUP = 11


def _dot(a, b):
    return jnp.dot(a.astype(BF16), b.astype(BF16), preferred_element_type=F32)


def _dot_nt(a, b):
    return lax.dot_general(a.astype(BF16), b.astype(BF16), (((1,), (1,)), ((), ())),
                           preferred_element_type=F32)


def _dot_tn(a, b):
    return lax.dot_general(a.astype(BF16), b.astype(BF16), (((0,), (0,)), ((), ())),
                           preferred_element_type=F32)


def _group_sum(x, m):
    xb = x.astype(BF16)
    bw = m.shape[0]
    parts = [jnp.dot(xb[:, c:c + bw], m, preferred_element_type=F32) for c in range(0, x.shape[1], bw)]
    return parts[0] if len(parts) == 1 else jnp.concatenate(parts, axis=1)


def _rms_norm(x, g):
    ms = jnp.mean(x * x, axis=-1, keepdims=True)
    return x * lax.rsqrt(ms + RMS_EPS) * g


def _const_spec(shape):
    nd = len(shape)
    return pl.BlockSpec(shape, lambda *_: (0,) * nd, pipeline_mode=pl.Buffered(1))


def _rwkv_kernel(p_ref, mu_ref, w0_ref, wa_ref, a0_ref, g2_ref, kk_ref, ka_ref, rk_ref,
                 lnw_ref, lnb_ref, gsum_ref, tril_ref, o_ref,
                 r_s, k_s, v_s, a_s, b_s, lw_s, y_s, state_s, slab_s):
    tt = p_ref.shape[1]
    W = RW_WIDTH
    C = RW_CHUNK

    @pl.when(pl.program_id(1) == 0)
    def _():
        state_s[...] = jnp.zeros_like(state_s)

    first_tile = pl.program_id(1) == 0
    blocks = []
    for blk in range(RW_COLS // LANES):
        sl = slice(blk * LANES, (blk + 1) * LANES)
        slab = slab_s.at[blk]
        slab[0:SUBLANES, :] = jnp.where(first_tile, 0.0, slab[tt:tt + SUBLANES, :])
        pb = p_ref[0, :, sl].astype(F32)
        slab[SUBLANES:SUBLANES + tt, :] = pb
        blocks.append(pb + (slab[SUBLANES - 1:SUBLANES - 1 + tt, :] - pb) * mu_ref[:, sl])
    p = jnp.concatenate(blocks, axis=1)

    r = p[:, 0:W]
    k = p[:, W:2 * W]
    v = p[:, 2 * W:3 * W]
    lane = lax.broadcasted_iota(jnp.int32, (1, LANES), 1)
    wa_in = p[:, 3 * W:3 * W + LANES]
    wa_in = jnp.where(lane < RW_DECAY_RANK, jnp.tanh(wa_in), wa_in)
    wa = _dot(wa_in, wa_ref[...])
    gd = p[:, 3 * W + LANES:3 * W + 2 * LANES]

    lw_s[...] = -math.exp(-0.5) * jax.nn.sigmoid(w0_ref[...] + wa[:, 0:W])
    a_lr = jax.nn.sigmoid(a0_ref[...] + wa[:, W:2 * W])

    gsum = gsum_ref[...]
    kk = k * kk_ref[...]
    kk = kk * lax.rsqrt(jnp.maximum(_group_sum(kk * kk, gsum), 1e-24))
    k = k * (1.0 + (a_lr - 1.0) * ka_ref[...])

    r_s[...] = r
    k_s[...] = k
    v_s[...] = v
    a_s[...] = -kk
    b_s[...] = kk * a_lr

    lane2 = lax.broadcasted_iota(jnp.int32, (C, LANES), 1)
    trow = lax.broadcasted_iota(jnp.int32, (C, LANES), 0)
    m0 = lane2 < RW_HEAD_DIM
    scol = jnp.where(m0, lane2, lane2 - RW_HEAD_DIM)
    strict = trow > scol
    incl = trow >= scol
    brow = lax.broadcasted_iota(jnp.int32, (LANES, LANES), 0) < RW_HEAD_DIM
    bcol = lax.broadcasted_iota(jnp.int32, (LANES, LANES), 1) < RW_HEAD_DIM
    blockmask = brow == bcol
    tril = tril_ref[...]

    def bd(x):
        z = jnp.zeros_like(x)
        return jnp.concatenate([jnp.where(m0, x, z), jnp.where(m0, z, x)], axis=0)

    n_pairs = RW_HEADS // 2
    grp = min(RW_GROUP, tt // C)
    gc = grp * C
    grow = lax.broadcasted_iota(jnp.int32, (gc, 1), 0)
    n_levels = int(math.log2(C))

    def prep_body(i, carry):
        rows = pl.ds(pl.multiple_of(i * gc, gc), gc)
        lw = lw_s[rows, :]
        h1 = lw.astype(BF16)
        h2 = (lw - h1.astype(F32)).astype(BF16)
        cb = tril.shape[0]
        cum = jnp.concatenate(
            [jnp.dot(tril, h1[q * cb:(q + 1) * cb], preferred_element_type=F32)
             + jnp.dot(tril, h2[q * cb:(q + 1) * cb], preferred_element_type=F32)
             for q in range(gc // cb)], axis=0)
        tot = cum[C - 1:C, :]
        for g in range(1, grp):
            tot = jnp.where(grow >= g * C, cum[(g + 1) * C - 1:(g + 1) * C, :], tot)
        w_inv = jnp.exp(-cum)
        w_rem = jnp.exp(tot - cum)
        rc = r_s[rows, :]
        kc = k_s[rows, :]
        bc = b_s[rows, :]
        v32c = v_s[rows, :]
        vc = v32c.astype(BF16)
        rt32 = rc * jnp.exp(cum)
        at32 = a_s[rows, :] * jnp.exp(cum - lw)
        rt = rt32.astype(BF16)
        at = at32.astype(BF16)
        bt = (bc * w_inv).astype(BF16)
        kt = (kc * w_inv).astype(BF16)
        bh = (bc * w_rem).astype(BF16)
        kh = (kc * w_rem).astype(BF16)

        units = [(g, j) for g in range(grp) for j in range(n_pairs)]
        blk = lambda arr, g, j: arr[g * C:(g + 1) * C, LANES * j:LANES * (j + 1)]
        zero = jnp.zeros((C, LANES), F32)
        stack2 = lambda x: jnp.concatenate([x, x], axis=0)
        zero2 = jnp.zeros((2 * C, LANES), F32)
        a_rb, a_rk, z, n_bd = {}, {}, {}, {}
        ak_bd, vw, a_st = {}, {}, {}
        for u in units:
            lhs = jnp.concatenate([blk(at, *u), blk(rt, *u)], axis=0)
            gram = _dot_nt(lhs, jnp.concatenate([bd(blk(bt, *u)), bd(blk(kt, *u))], axis=0))
            a_ab = jnp.where(strict, gram[0:C, 0:LANES], zero)
            a_ak = jnp.where(strict, gram[0:C, LANES:2 * LANES], zero)
            n_bd[u] = jnp.where(blockmask, stack2(a_ab), zero2).astype(BF16)
            ak_bd[u] = jnp.where(blockmask, stack2(a_ak), zero2).astype(BF16)
            a_rb[u] = jnp.where(incl, gram[C:2 * C, 0:LANES], zero).astype(BF16)
            a_rk[u] = jnp.where(incl, gram[C:2 * C, LANES:2 * LANES], zero).astype(BF16)
            v32 = blk(v32c, *u)
            vw[u] = jnp.concatenate([pltpu.roll(v32, RW_HEAD_DIM, 1), v32], axis=0).astype(BF16)
            a32 = blk(at32, *u)
            a_st[u] = jnp.concatenate([a32, pltpu.roll(a32, RW_HEAD_DIM, 1)], axis=0)
        for u in units:
            akv = jnp.dot(ak_bd[u], vw[u], preferred_element_type=F32)
            z[u] = jnp.where(bcol, a_st[u], akv)
        for lvl in range(n_levels):
            last = lvl == n_levels - 1
            for u in units:
                zb = z[u].astype(BF16)
                rhs = zb if last else jnp.concatenate([zb, n_bd[u]], axis=1)
                comb = jnp.dot(n_bd[u], rhs, preferred_element_type=F32)
                z[u] = z[u] + comb[:, 0:LANES]
                if not last:
                    n_bd[u] = comb[:, LANES:2 * LANES].astype(BF16)

        def finish(u):
            top = z[u][0:C]
            bot = z[u][C:2 * C]
            apb = jnp.where(m0, top, pltpu.roll(bot, RW_HEAD_DIM, 1)).astype(BF16)
            uvb = jnp.where(m0, pltpu.roll(top, RW_HEAD_DIM, 1), bot).astype(BF16)
            out = jnp.dot(a_rb[u], jnp.concatenate([bd(apb), bd(uvb)], axis=1),
                          preferred_element_type=F32)
            rp = (blk(rt32, *u) + out[:, 0:LANES]).astype(BF16)
            yv = out[:, LANES:2 * LANES] + jnp.dot(a_rk[u], bd(blk(vc, *u)), preferred_element_type=F32)
            x = _dot_tn(apb, blk(bh, *u))
            x = jnp.where(blockmask, x, jnp.zeros_like(x)).astype(BF16)
            e = _dot_tn(jnp.concatenate([uvb, blk(vc, *u)], axis=0),
                        jnp.concatenate([blk(bh, *u), blk(kh, *u)], axis=0))
            e = jnp.where(blockmask, e, jnp.zeros_like(e))
            return rp, yv, x, e

        s_cur = [state_s[j] for j in range(n_pairs)]

        def scan_step(g, fin):
            crow = pl.ds(pl.multiple_of(i * gc + g * C, C), C)
            w_tot = jnp.exp(cum[(g + 1) * C - 1:(g + 1) * C, :])
            for j in range(n_pairs):
                sl = slice(LANES * j, LANES * (j + 1))
                rp, yv, x, e = fin[j]
                sb = s_cur[j].astype(BF16)
                y_s[crow, sl] = _dot_nt(rp, sb) + yv
                s_cur[j] = s_cur[j] * w_tot[:, sl] + e + jnp.dot(sb, x, preferred_element_type=F32)

        pending = None
        for g in range(grp):
            fin = [finish((g, j)) for j in range(n_pairs)]
            if pending is not None:
                scan_step(*pending)
            pending = (g, fin)
        scan_step(*pending)
        for j in range(n_pairs):
            state_s[j] = s_cur[j]
        return carry

    lax.fori_loop(0, tt // gc, prep_body, 0)

    g = _dot(jax.nn.sigmoid(gd), g2_ref[...])
    bonus = _group_sum(r_s[...] * k_s[...] * rk_ref[...], gsum) * v_s[...]
    y = y_s[...]
    inv_n = 1.0 / RW_HEAD_DIM
    mean = _group_sum(y, gsum) * inv_n
    yc = y - mean
    var = _group_sum(yc * yc, gsum) * inv_n
    yn = yc * lax.rsqrt(var + RW_GN_EPS) * lnw_ref[...] + lnb_ref[...]
    o_ref[0] = ((yn + bonus) * g).astype(o_ref.dtype)


def _rwkv_mixer(p_rw, mu, w0, wa, a0, g2, k_k, k_a, r_k, lnw, lnb, tt):
    b, t, _ = p_rw.shape
    gsum = jnp.asarray(np.kron(np.eye(MXU_DIM // RW_HEAD_DIM), np.ones((RW_HEAD_DIM, RW_HEAD_DIM))), BF16)
    tril = jnp.asarray(np.kron(np.eye(RW_CUM_CHUNKS), np.tril(np.ones((RW_CHUNK, RW_CHUNK)))), BF16)
    n_pairs = RW_HEADS // 2
    vec = lambda: _const_spec((1, RW_WIDTH))
    seq_scratch = lambda: pltpu.VMEM((tt, RW_WIDTH), F32)
    return pl.pallas_call(
        _rwkv_kernel,
        grid=(b, t // tt),
        in_specs=[pl.BlockSpec((1, tt, RW_COLS), lambda i, j: (i, j, 0)),
                  _const_spec((1, RW_COLS)), vec(), _const_spec(wa.shape), vec(),
                  _const_spec(g2.shape), vec(), vec(), vec(), vec(), vec(),
                  _const_spec(gsum.shape), _const_spec(tril.shape)],
        out_specs=pl.BlockSpec((1, tt, RW_WIDTH), lambda i, j: (i, j, 0)),
        out_shape=jax.ShapeDtypeStruct((b, t, RW_WIDTH), BF16),
        scratch_shapes=[seq_scratch() for _ in range(7)]
        + [pltpu.VMEM((n_pairs, LANES, LANES), F32),
           pltpu.VMEM((RW_COLS // LANES, SUBLANES + tt, LANES), F32)],
        compiler_params=pltpu.CompilerParams(dimension_semantics=("arbitrary", "arbitrary"),
                                             vmem_limit_bytes=VMEM_LIMIT),
        name="rwkv7_mixer",
    )(p_rw, mu, w0, wa, a0, g2, k_k, k_a, r_k, lnw, lnb, gsum, tril)


def _proj_retention_kernel(tiles_per_row, x_ref, g_ref, wrw_ref, wret_ref, wg_ref, pos_ref,
                           prw_ref, pg_ref, o_ref, p_s, vg_s, q_s, k_s, o_s, dmask_s, state_s):
    tt = x_ref.shape[0]
    C = RET_CHUNK
    QW = RET_QK_WIDTH
    half = RET_QK_DIM // 2
    n_pairs = RET_HEADS // 2
    step = pl.program_id(0)

    @pl.when(step == 0)
    def _():
        p_s[...] = jnp.zeros_like(p_s)

    @pl.when((lax.rem(step, tiles_per_row) == 1 % tiles_per_row) | (step == 0))
    def _():
        state_s[...] = jnp.zeros_like(state_s)

    hn = _rms_norm(x_ref[...], g_ref[...]).astype(BF16)
    vg_s[...] = p_s[:, 2 * QW:]

    lane = lax.broadcasted_iota(jnp.int32, (1, LANES), 1)
    freq = (lane & (half - 1)).astype(F32)
    inv = jnp.exp(freq * (-math.log(ROPE_BASE) / half))
    first = (lane & (RET_QK_DIM - 1)) < half
    n_copies = LANES // half
    rp = tt // n_copies
    lane_blk = lane >> int(math.log2(half))
    pos = pos_ref[...].astype(F32)
    ang = pos[0:rp, :] * inv
    for g in range(1, n_copies):
        ang = jnp.where(lane_blk == g, pos[g * rp:(g + 1) * rp, :] * inv, ang)

    def spread(tab):
        rolled = [tab] + [pltpu.roll(tab, half * k, 1) for k in range(1, n_copies)]
        groups = []
        for g in range(n_copies):
            full = rolled[(0 - g) % n_copies]
            for blk in range(1, n_copies):
                full = jnp.where(lane_blk == blk, rolled[(blk - g) % n_copies], full)
            groups.append(full)
        return jnp.concatenate(groups, axis=0)

    cos = spread(jnp.cos(ang))
    sin = spread(jnp.sin(ang))
    sin = jnp.where(first, -sin, sin)

    def rope(z):
        swapped = jnp.where(first, pltpu.roll(z, LANES - half, 1), pltpu.roll(z, half, 1))
        return z * cos + swapped * sin

    for j in range(n_pairs):
        sl = slice(LANES * j, LANES * (j + 1))
        q_s[:, sl] = rope(p_s[:, sl].astype(F32))
        k_s[:, sl] = rope(p_s[:, QW + LANES * j:QW + LANES * (j + 1)].astype(F32)) * (RET_QK_DIM ** -0.5)

    prw_ref[...] = jnp.dot(hn, wrw_ref[...], preferred_element_type=F32).astype(BF16)

    lane2 = lax.broadcasted_iota(jnp.int32, (C, 2 * LANES), 1)
    trow2 = lax.broadcasted_iota(jnp.int32, (C, 2 * LANES), 0)
    lane1 = lax.broadcasted_iota(jnp.int32, (C, LANES), 1)
    trow1 = lax.broadcasted_iota(jnp.int32, (C, LANES), 0).astype(F32)
    m0 = lane1 < RET_QK_DIM
    head_of_lane = int(math.log2(LANES))
    head_of_qk = int(math.log2(RET_QK_DIM))
    q_decay, k_decay, chunk_decay = [], [], []
    for j in range(n_pairs):
        lg = lambda hsel: jnp.log1p(-jnp.exp2(-5.0 - (2 * j + hsel).astype(F32)))
        lg2 = lg(lane2 >> head_of_lane)
        rel = (trow2 - (lane2 & (LANES - 1))).astype(F32)
        dmask_s[j] = jnp.where(rel >= 0, jnp.exp(jnp.maximum(rel, 0.0) * lg2), 0.0)
        lg1 = lg(lane1 >> head_of_qk)
        q_decay.append(jnp.exp((trow1 + 1.0) * lg1))
        k_decay.append(jnp.exp((C - 1.0 - trow1) * lg1))
        srow = lax.broadcasted_iota(jnp.int32, (LANES, LANES), 0) >> head_of_qk
        chunk_decay.append(jnp.exp(C * lg(srow)))

    def bd(x):
        z = jnp.zeros_like(x)
        return jnp.concatenate([jnp.where(m0, x, z), jnp.where(m0, z, x)], axis=0)

    units = [(c, j) for c in range(tt // C) for j in range(n_pairs)]
    hrow = lax.broadcasted_iota(jnp.int32, (LANES, LANES), 0) < RET_QK_DIM
    vcols = lambda j: slice(2 * LANES * j, 2 * LANES * (j + 1))
    scores, incr, qd = {}, {}, {}
    for c, j in units:
        rows = slice(c * C, (c + 1) * C)
        sl = slice(LANES * j, LANES * (j + 1))
        qj = q_s[rows, sl]
        kj = k_s[rows, sl]
        scores[c, j] = (_dot_nt(qj, bd(kj.astype(BF16))) * dmask_s[j]).astype(BF16)
        qd[c, j] = (qj * q_decay[j]).astype(BF16)
        x = _dot_tn(kj * k_decay[j], vg_s[rows, vcols(j)])
        incr[c, j] = jnp.where(hrow, x[:, 0:LANES], x[:, LANES:2 * LANES])
    state = {}
    for j in range(n_pairs):
        r_cur = state_s[j]
        for c in range(tt // C):
            state[c, j] = r_cur.astype(BF16)
            r_cur = r_cur * chunk_decay[j] + incr[c, j]
        state_s[j] = r_cur
    pg_ref[...] = jnp.dot(hn, wg_ref[...], preferred_element_type=F32).astype(BF16)
    for c, j in units:
        rows = slice(c * C, (c + 1) * C)
        vpair = vg_s[rows, vcols(j)]
        zq = jnp.zeros_like(qd[c, j])
        for hh in range(2):
            h = 2 * j + hh
            qm = jnp.where(m0, qd[c, j], zq) if hh == 0 else jnp.where(m0, zq, qd[c, j])
            lhs = jnp.concatenate([scores[c, j][:, LANES * hh:LANES * (hh + 1)], qm], axis=1)
            rhs = jnp.concatenate([vpair[:, LANES * hh:LANES * (hh + 1)], state[c, j]], axis=0)
            o_s[rows, LANES * h:LANES * (h + 1)] = jnp.dot(lhs, rhs, preferred_element_type=F32)

    def project_ret(piece):
        cols = slice(piece * RET_COLS // 3, (piece + 1) * RET_COLS // 3)
        p_s[:, cols] = jnp.dot(hn, wret_ref[:, cols], preferred_element_type=F32).astype(BF16)

    blocks = [slice(c0, c0 + RET_V_DIM) for c0 in range(0, RET_V_WIDTH, RET_V_DIM)]
    project_ret(0)
    means = [jnp.mean(o_s[:, sl], axis=-1, keepdims=True) for sl in blocks]
    project_ret(1)
    centred = [o_s[:, sl] - m for sl, m in zip(blocks, means)]
    variances = [jnp.mean(oc * oc, axis=-1, keepdims=True) for oc in centred]
    project_ret(2)
    for sl, oc, var in zip(blocks, centred, variances):
        gate = vg_s[:, RET_V_WIDTH + sl.start:RET_V_WIDTH + sl.stop].astype(F32)
        o_ref[:, sl] = (gate * jax.nn.sigmoid(gate) * oc * lax.rsqrt(var + RET_GN_EPS)).astype(o_ref.dtype)


def _proj_retention(x2, g, w_rw, w_ret, w_gate, pos2, tt, tiles_per_row):
    n = x2.shape[0]
    n_tiles = n // tt
    n_pairs = RET_HEADS // 2
    cur = lambda w: pl.BlockSpec((tt, w), lambda i: (jnp.minimum(i, n_tiles - 1), 0))
    prev = lambda w: pl.BlockSpec((tt, w), lambda i: (jnp.maximum(i - 1, 0), 0))
    return pl.pallas_call(
        functools.partial(_proj_retention_kernel, tiles_per_row),
        grid=(n_tiles + 1,),
        in_specs=[cur(D_MODEL), _const_spec((1, D_MODEL)), _const_spec(w_rw.shape),
                  _const_spec(w_ret.shape), _const_spec(w_gate.shape), prev(1)],
        out_specs=[cur(RW_COLS), cur(2 * D_MODEL), prev(RET_V_WIDTH)],
        out_shape=[jax.ShapeDtypeStruct((n, RW_COLS), BF16),
                   jax.ShapeDtypeStruct((n, 2 * D_MODEL), BF16),
                   jax.ShapeDtypeStruct((n, RET_V_WIDTH), BF16)],
        scratch_shapes=[pltpu.VMEM((tt, RET_COLS), BF16), pltpu.VMEM((tt, 2 * RET_V_WIDTH), BF16),
                        pltpu.VMEM((tt, RET_QK_WIDTH), F32), pltpu.VMEM((tt, RET_QK_WIDTH), F32),
                        pltpu.VMEM((tt, RET_V_WIDTH), F32),
                        pltpu.VMEM((n_pairs, RET_CHUNK, 2 * LANES), F32),
                        pltpu.VMEM((n_pairs, LANES, LANES), F32)],
        compiler_params=pltpu.CompilerParams(dimension_semantics=("arbitrary",),
                                             vmem_limit_bytes=VMEM_LIMIT),
        name="projection_retention",
    )(x2, g, w_rw, w_ret, w_gate, pos2)


def _merge_kernel(x_ref, yrw_ref, yret_ref, pg_ref, wbrw_ref, wbret_ref, wout_ref,
                  npost_ref, nffn_ref, h_ref, hn_ref):
    tm = x_ref.shape[0]
    subs = [slice(r, r + tm // MERGE_SPLIT) for r in range(0, tm, tm // MERGE_SPLIT)]
    branches = [(jnp.dot(yrw_ref[rs, :], wbrw_ref[...], preferred_element_type=F32),
                 jnp.dot(yret_ref[rs, :], wbret_ref[...], preferred_element_type=F32)) for rs in subs]

    def finish(rs, mx):
        h = x_ref[rs, :] + _rms_norm(mx, npost_ref[...])
        h_ref[rs, :] = h
        hn_ref[rs, :] = _rms_norm(h, nffn_ref[...]).astype(BF16)

    pending = None
    for rs, (b_rw, b_ret) in zip(subs, branches):
        g_rw = jax.nn.sigmoid(pg_ref[rs, 0:D_MODEL].astype(F32))
        g_ret = jax.nn.sigmoid(pg_ref[rs, D_MODEL:2 * D_MODEL].astype(F32))
        mx = _dot(g_rw * b_rw + g_ret * b_ret, wout_ref[...])
        if pending is not None:
            finish(*pending)
        pending = (rs, mx)
    finish(*pending)


def _merge(x2, y_rw, y_ret, p_gate, wb_rw, wb_ret, w_out, n_post, n_ffn, tm):
    n = x2.shape[0]
    row = lambda w: pl.BlockSpec((tm, w), lambda i: (i, 0))
    return pl.pallas_call(
        _merge_kernel,
        grid=(n // tm,),
        in_specs=[row(D_MODEL), row(RW_WIDTH), row(RET_V_WIDTH), row(2 * D_MODEL),
                  _const_spec(wb_rw.shape), _const_spec(wb_ret.shape), _const_spec(w_out.shape),
                  _const_spec((1, D_MODEL)), _const_spec((1, D_MODEL))],
        out_specs=[row(D_MODEL), row(D_MODEL)],
        out_shape=[jax.ShapeDtypeStruct((n, D_MODEL), F32),
                   jax.ShapeDtypeStruct((n, D_MODEL), BF16)],
        compiler_params=pltpu.CompilerParams(dimension_semantics=("arbitrary",),
                                             vmem_limit_bytes=VMEM_LIMIT),
        name="merge_out_projection",
    )(x2, y_rw, y_ret, p_gate, wb_rw, wb_ret, w_out, n_post, n_ffn)


def _gelu_tanh(x):
    k0 = -2.0 * math.sqrt(2.0 / math.pi) * math.log2(math.e)
    k1 = k0 * 0.044715
    e = jnp.exp2(x * (k1 * (x * x) + k0))
    return x * (1.0 / (1.0 + e))


def _ffn_kernel(hn_ref, h_ref, wup_ref, cw_ref, cb_ref, wdn_ref, npost_ref, o_ref, slab_s, acc_s):
    tt = hn_ref.shape[1]
    fc = FF_CHUNK
    first_tile = pl.program_id(1) == 0

    hn = hn_ref[0]

    def conv(u, col):
        outs = []
        for blk in range(fc // LANES):
            c0 = col + blk * LANES
            slab = slab_s.at[c0 // LANES]
            prev = jnp.where(first_tile, 0.0, slab[tt:tt + SUBLANES, :])
            slab[0:SUBLANES, :] = prev
            ub = u[:, blk * LANES:(blk + 1) * LANES]
            slab[SUBLANES:SUBLANES + tt, :] = ub
            s1 = slab[SUBLANES - 1:SUBLANES - 1 + tt, :]
            s2 = slab[SUBLANES - 2:SUBLANES - 2 + tt, :]
            cw = cw_ref[:, c0:c0 + LANES]
            outs.append(ub * cw[2:3, :] + s1 * cw[1:2, :] + s2 * cw[0:1, :] + cb_ref[:, c0:c0 + LANES])
        return jnp.concatenate(outs, axis=1)

    def up(j):
        cg = j * fc
        cv = D_FF + j * fc
        return (jnp.dot(hn, wup_ref[:, cg:cg + fc], preferred_element_type=F32),
                jnp.dot(hn, wup_ref[:, cv:cv + fc], preferred_element_type=F32))

    n_chunks = D_FF // fc
    u_next = up(0)
    acts = []
    for j in range(n_chunks):
        cg = j * fc
        cv = D_FF + j * fc
        u_gate, u_val = u_next
        if j + 1 < n_chunks:
            u_next = up(j + 1)
        gate = conv(u_gate, cg)
        val = conv(u_val, cv)
        acts.append((_gelu_tanh(gate) * val).astype(BF16))
        if len(acts) == FF_DOWN_GROUP or j == n_chunks - 1:
            k0 = (j + 1 - len(acts)) * fc
            act = acts[0] if len(acts) == 1 else jnp.concatenate(acts, axis=1)
            part = jnp.dot(act, wdn_ref[k0:k0 + len(acts) * fc, :], preferred_element_type=F32)
            if k0 == 0:
                acc_s[...] = part
            else:
                acc_s[...] += part
            acts = []

    o_ref[0] = h_ref[0] + _rms_norm(acc_s[...], npost_ref[...])


def _conv_ffn(hn3, h3, w_up, conv_w, conv_b, w_down, n_post, tt):
    b, t, _ = hn3.shape
    tile = lambda: pl.BlockSpec((1, tt, D_MODEL), lambda i, j: (i, j, 0))
    return pl.pallas_call(
        _ffn_kernel,
        grid=(b, t // tt),
        in_specs=[tile(), tile(), _const_spec(w_up.shape), _const_spec(conv_w.shape),
                  _const_spec(conv_b.shape), _const_spec(w_down.shape), _const_spec((1, D_MODEL))],
        out_specs=tile(),
        out_shape=jax.ShapeDtypeStruct((b, t, D_MODEL), F32),
        scratch_shapes=[pltpu.VMEM((2 * D_FF // LANES, SUBLANES + tt, LANES), F32),
                        pltpu.VMEM((tt, D_MODEL), F32)],
        compiler_params=pltpu.CompilerParams(dimension_semantics=("arbitrary", "arbitrary"),
                                             vmem_limit_bytes=VMEM_LIMIT),
        name="conv_ffn",
    )(hn3, h3, w_up, conv_w, conv_b, w_down, n_post)


def kernel(x, positions, norm_mix_pre, norm_mix_post, norm_ffn_pre, norm_ffn_post, w_in, rw_mu, rw_w0, rw_w2, rw_a0, rw_a2, rw_g2, rw_k_k, rw_k_a, rw_r_k, rw_lnx_w, rw_lnx_b, w_branch_rw, w_branch_ret, w_out, ffn_w_up, ffn_conv_w, ffn_conv_b, ffn_w_down):
    b, t, d = x.shape
    assert d == D_MODEL and norm_mix_pre.shape[0] == 1
    n = b * t
    tm = min(TOKEN_TILE, n)
    tt = min(SEQ_TILE, t)
    assert n % tm == 0 and t % tt == 0 and tt % RET_CHUNK == 0 and t % min(RW_SEQ_TILE, t) == 0
    assert ffn_conv_w.shape[1] == CONV_WIDTH == 3
    vec = lambda a: a[0].reshape(1, -1).astype(F32)

    w_in_b = w_in[0].astype(BF16)
    w_rw = w_in_b[:, 0:RW_COLS]
    w_ret = w_in_b[:, RW_COLS:RW_COLS + RET_COLS]
    w_gate = w_in_b[:, RW_COLS + RET_COLS:]
    zeros = jnp.zeros((RW_DECAY_RANK, RW_WIDTH), F32)
    wa = jnp.concatenate([jnp.concatenate([rw_w2[0], zeros], axis=1),
                          jnp.concatenate([zeros, rw_a2[0]], axis=1)], axis=0).astype(BF16)

    x2 = x.reshape(n, d)
    p_rw, p_gate, y_ret = _proj_retention(x2, vec(norm_mix_pre), w_rw, w_ret, w_gate,
                                          positions.reshape(n, 1), tt, t // tt)

    y_rw = _rwkv_mixer(p_rw.reshape(b, t, RW_COLS), vec(rw_mu), vec(rw_w0), wa, vec(rw_a0),
                       rw_g2[0].astype(BF16), vec(rw_k_k), vec(rw_k_a), vec(rw_r_k),
                       vec(rw_lnx_w), vec(rw_lnx_b), min(RW_SEQ_TILE, t))

    h, hn = _merge(x2, y_rw.reshape(n, RW_WIDTH), y_ret, p_gate,
                   w_branch_rw[0].astype(BF16), w_branch_ret[0].astype(BF16), w_out[0].astype(BF16),
                   vec(norm_mix_post), vec(norm_ffn_pre), tm)

    out = _conv_ffn(hn.reshape(b, t, d), h.reshape(b, t, d), ffn_w_up[0].astype(BF16),
                    ffn_conv_w[0].astype(F32), vec(ffn_conv_b), ffn_w_down[0].astype(BF16),
                    vec(norm_ffn_post), tt)
    return out.astype(x.dtype)
```

```python
import functools
import math

import numpy as np
import jax
import jax.numpy as jnp
from jax import lax
from jax.experimental import pallas as pl
from jax.experimental.pallas import tpu as pltpu

F32 = jnp.float32
BF16 = jnp.bfloat16

D_MODEL = 1024
RW_HEAD_DIM = 64
RW_WIDTH = 512
RW_HEADS = 8
RW_DECAY_RANK = 64
RW_ICLR_RANK = 64
RW_GATE_RANK = 128
RW_COLS = 3 * RW_WIDTH + RW_DECAY_RANK + RW_ICLR_RANK + RW_GATE_RANK
RW_GN_EPS = 64e-5
RW_CHUNK = 64
RW_GROUP = 16
RW_CUM_CHUNKS = 2
RET_QK_DIM = 64
RET_QK_WIDTH = 512
RET_HEADS = 8
RET_V_DIM = 128
RET_V_WIDTH = 1024
RET_COLS = 2 * RET_QK_WIDTH + 2 * RET_V_WIDTH
RET_CHUNK = 128
ROPE_BASE = 10000.0
RET_GN_EPS = 1e-5
D_FF = 2816
CONV_WIDTH = 3
RMS_EPS = 1e-6

LANES = 128
SUBLANES = 8
MXU_DIM = 256
VMEM_LIMIT = 56 * 1024 * 1024
TOKEN_TILE = 1024
SEQ_TILE = 512
RW_SEQ_TILE = 1024
MERGE_SPLIT = 8
FF_CHUNK = 256
FF_DOWN_GROUP = 11


def _dot(a, b):
    return jnp.dot(a.astype(BF16), b.astype(BF16), preferred_element_type=F32)


def _dot_nt(a, b):
    return lax.dot_general(a.astype(BF16), b.astype(BF16), (((1,), (1,)), ((), ())),
                           preferred_element_type=F32)


def _dot_tn(a, b):
    return lax.dot_general(a.astype(BF16), b.astype(BF16), (((0,), (0,)), ((), ())),
                           preferred_element_type=F32)


def _head_sum(x):
    lane = lax.broadcasted_iota(jnp.int32, (1, LANES), 1)
    low = lane < RW_HEAD_DIM
    parts = []
    for c in range(0, x.shape[1], LANES):
        xb = x[:, c:c + LANES]
        s_lo = jnp.sum(jnp.where(low, xb, 0.0), axis=-1, keepdims=True)
        s_hi = jnp.sum(jnp.where(low, 0.0, xb), axis=-1, keepdims=True)
        parts.append(jnp.where(low, s_lo, s_hi))
    return jnp.concatenate(parts, axis=1)


def _rms_norm(x, g):
    ms = jnp.mean(x * x, axis=-1, keepdims=True)
    return x * lax.rsqrt(ms + RMS_EPS) * g


def _const_spec(shape):
    nd = len(shape)
    return pl.BlockSpec(shape, lambda *_: (0,) * nd, pipeline_mode=pl.Buffered(1))


def _rwkv_kernel(p_ref, mu_ref, w0_ref, wa_ref, a0_ref, g2_ref, kk_ref, ka_ref, rk_ref,
                 lnw_ref, lnb_ref, tril_ref, o_ref,
                 r_s, k_s, v_s, a_s, b_s, lw_s, y_s, state_s, slab_s):
    tt = p_ref.shape[1]
    W = RW_WIDTH
    C = RW_CHUNK

    @pl.when(pl.program_id(1) == 0)
    def _():
        state_s[...] = jnp.zeros_like(state_s)

    first_tile = pl.program_id(1) == 0
    blocks = []
    for blk in range(RW_COLS // LANES):
        sl = slice(blk * LANES, (blk + 1) * LANES)
        slab = slab_s.at[blk]
        slab[0:SUBLANES, :] = jnp.where(first_tile, 0.0, slab[tt:tt + SUBLANES, :])
        pb = p_ref[0, :, sl].astype(F32)
        slab[SUBLANES:SUBLANES + tt, :] = pb
        blocks.append(pb + (slab[SUBLANES - 1:SUBLANES - 1 + tt, :] - pb) * mu_ref[:, sl])
    p = jnp.concatenate(blocks, axis=1)

    r = p[:, 0:W]
    k = p[:, W:2 * W]
    v = p[:, 2 * W:3 * W]
    lane = lax.broadcasted_iota(jnp.int32, (1, LANES), 1)
    wa_in = p[:, 3 * W:3 * W + LANES]
    wa_in = jnp.where(lane < RW_DECAY_RANK, jnp.tanh(wa_in), wa_in)
    wa = _dot(wa_in, wa_ref[...])
    gd = p[:, 3 * W + LANES:3 * W + 2 * LANES]

    lw_s[...] = -math.exp(-0.5) * jax.nn.sigmoid(w0_ref[...] + wa[:, 0:W])
    a_lr = jax.nn.sigmoid(a0_ref[...] + wa[:, W:2 * W])

    kk = k * kk_ref[...]
    kk = kk * lax.rsqrt(jnp.maximum(_head_sum(kk * kk), 1e-24))
    k = k * (1.0 + (a_lr - 1.0) * ka_ref[...])

    r_s[...] = r
    k_s[...] = k
    v_s[...] = v
    a_s[...] = -kk
    b_s[...] = kk * a_lr

    lane2 = lax.broadcasted_iota(jnp.int32, (C, LANES), 1)
    trow = lax.broadcasted_iota(jnp.int32, (C, LANES), 0)
    m0 = lane2 < RW_HEAD_DIM
    scol = jnp.where(m0, lane2, lane2 - RW_HEAD_DIM)
    strict = trow > scol
    incl = trow >= scol
    brow = lax.broadcasted_iota(jnp.int32, (LANES, LANES), 0) < RW_HEAD_DIM
    bcol = lax.broadcasted_iota(jnp.int32, (LANES, LANES), 1) < RW_HEAD_DIM
    blockmask = brow == bcol
    tril = tril_ref[...]

    def bd(x):
        z = jnp.zeros_like(x)
        return jnp.concatenate([jnp.where(m0, x, z), jnp.where(m0, z, x)], axis=0)

    n_pairs = RW_HEADS // 2
    grp = min(RW_GROUP, tt // C)
    gc = grp * C
    grow = lax.broadcasted_iota(jnp.int32, (gc, 1), 0)
    n_levels = int(math.log2(C))

    def prep_body(i, carry):
        rows = pl.ds(pl.multiple_of(i * gc, gc), gc)
        lw = lw_s[rows, :]
        h1 = lw.astype(BF16)
        h2 = (lw - h1.astype(F32)).astype(BF16)
        cb = tril.shape[0]
        cum = jnp.concatenate(
            [jnp.dot(tril, h1[q * cb:(q + 1) * cb], preferred_element_type=F32)
             + jnp.dot(tril, h2[q * cb:(q + 1) * cb], preferred_element_type=F32)
             for q in range(gc // cb)], axis=0)
        tot = cum[C - 1:C, :]
        for g in range(1, grp):
            tot = jnp.where(grow >= g * C, cum[(g + 1) * C - 1:(g + 1) * C, :], tot)
        w_inv = jnp.exp(-cum)
        w_rem = jnp.exp(tot - cum)
        rc = r_s[rows, :]
        kc = k_s[rows, :]
        bc = b_s[rows, :]
        v32c = v_s[rows, :]
        vc = v32c.astype(BF16)
        rt32 = rc * jnp.exp(cum)
        at32 = a_s[rows, :] * jnp.exp(cum - lw)
        rt = rt32.astype(BF16)
        at = at32.astype(BF16)
        bt = (bc * w_inv).astype(BF16)
        kt = (kc * w_inv).astype(BF16)
        bh = (bc * w_rem).astype(BF16)
        kh = (kc * w_rem).astype(BF16)

        units = [(g, j) for g in range(grp) for j in range(n_pairs)]
        blk = lambda arr, g, j: arr[g * C:(g + 1) * C, LANES * j:LANES * (j + 1)]
        zero = jnp.zeros((C, LANES), F32)
        stack2 = lambda x: jnp.concatenate([x, x], axis=0)
        zero2 = jnp.zeros((2 * C, LANES), F32)
        a_rb, a_rk, z, n_bd = {}, {}, {}, {}
        ak_bd, vw, a_st = {}, {}, {}
        for u in units:
            lhs = jnp.concatenate([blk(at, *u), blk(rt, *u)], axis=0)
            gram = _dot_nt(lhs, jnp.concatenate([bd(blk(bt, *u)), bd(blk(kt, *u))], axis=0))
            a_ab = jnp.where(strict, gram[0:C, 0:LANES], zero)
            a_ak = jnp.where(strict, gram[0:C, LANES:2 * LANES], zero)
            n_bd[u] = jnp.where(blockmask, stack2(a_ab), zero2).astype(BF16)
            ak_bd[u] = jnp.where(blockmask, stack2(a_ak), zero2).astype(BF16)
            a_rb[u] = jnp.where(incl, gram[C:2 * C, 0:LANES], zero).astype(BF16)
            a_rk[u] = jnp.where(incl, gram[C:2 * C, LANES:2 * LANES], zero).astype(BF16)
            v32 = blk(v32c, *u)
            vw[u] = jnp.concatenate([pltpu.roll(v32, RW_HEAD_DIM, 1), v32], axis=0).astype(BF16)
            a32 = blk(at32, *u)
            a_st[u] = jnp.concatenate([a32, pltpu.roll(a32, RW_HEAD_DIM, 1)], axis=0)
        for u in units:
            akv = jnp.dot(ak_bd[u], vw[u], preferred_element_type=F32)
            z[u] = jnp.where(bcol, a_st[u], akv)
        for lvl in range(n_levels):
            last = lvl == n_levels - 1
            for u in units:
                zb = z[u].astype(BF16)
                rhs = zb if last else jnp.concatenate([zb, n_bd[u]], axis=1)
                comb = jnp.dot(n_bd[u], rhs, preferred_element_type=F32)
                z[u] = z[u] + comb[:, 0:LANES]
                if not last:
                    n_bd[u] = comb[:, LANES:2 * LANES].astype(BF16)

        def finish(u):
            top = z[u][0:C]
            bot = z[u][C:2 * C]
            apb = jnp.where(m0, top, pltpu.roll(bot, RW_HEAD_DIM, 1)).astype(BF16)
            uvb = jnp.where(m0, pltpu.roll(top, RW_HEAD_DIM, 1), bot).astype(BF16)
            out = jnp.dot(a_rb[u], jnp.concatenate([bd(apb), bd(uvb)], axis=1),
                          preferred_element_type=F32)
            rp = (blk(rt32, *u) + out[:, 0:LANES]).astype(BF16)
            yv = out[:, LANES:2 * LANES] + jnp.dot(a_rk[u], bd(blk(vc, *u)), preferred_element_type=F32)
            x = _dot_tn(apb, blk(bh, *u))
            x = jnp.where(blockmask, x, jnp.zeros_like(x)).astype(BF16)
            e = _dot_tn(jnp.concatenate([uvb, blk(vc, *u)], axis=0),
                        jnp.concatenate([blk(bh, *u), blk(kh, *u)], axis=0))
            e = jnp.where(blockmask, e, jnp.zeros_like(e))
            return rp, yv, x, e

        s_cur = [state_s[j] for j in range(n_pairs)]

        def scan_step(g, fin):
            crow = pl.ds(pl.multiple_of(i * gc + g * C, C), C)
            w_tot = jnp.exp(cum[(g + 1) * C - 1:(g + 1) * C, :])
            for j in range(n_pairs):
                sl = slice(LANES * j, LANES * (j + 1))
                rp, yv, x, e = fin[j]
                sb = s_cur[j].astype(BF16)
                y_s[crow, sl] = _dot_nt(rp, sb) + yv
                s_cur[j] = s_cur[j] * w_tot[:, sl] + e + jnp.dot(sb, x, preferred_element_type=F32)

        pending = None
        for g in range(grp):
            fin = [finish((g, j)) for j in range(n_pairs)]
            if pending is not None:
                scan_step(*pending)
            pending = (g, fin)
        scan_step(*pending)
        for j in range(n_pairs):
            state_s[j] = s_cur[j]
        return carry

    lax.fori_loop(0, tt // gc, prep_body, 0)

    g = _dot(jax.nn.sigmoid(gd), g2_ref[...])
    bonus = _head_sum(r_s[...] * k_s[...] * rk_ref[...]) * v_s[...]
    y = y_s[...]
    inv_n = 1.0 / RW_HEAD_DIM
    mean = _head_sum(y) * inv_n
    yc = y - mean
    var = _head_sum(yc * yc) * inv_n
    yn = yc * lax.rsqrt(var + RW_GN_EPS) * lnw_ref[...] + lnb_ref[...]
    o_ref[0] = ((yn + bonus) * g).astype(o_ref.dtype)


def _rwkv_mixer(p_rw, mu, w0, wa, a0, g2, k_k, k_a, r_k, lnw, lnb, tt):
    b, t, _ = p_rw.shape
    tril = jnp.asarray(np.kron(np.eye(RW_CUM_CHUNKS), np.tril(np.ones((RW_CHUNK, RW_CHUNK)))), BF16)
    n_pairs = RW_HEADS // 2
    vec = lambda: _const_spec((1, RW_WIDTH))
    seq_scratch = lambda: pltpu.VMEM((tt, RW_WIDTH), F32)
    return pl.pallas_call(
        _rwkv_kernel,
        grid=(b, t // tt),
        in_specs=[pl.BlockSpec((1, tt, RW_COLS), lambda i, j: (i, j, 0)),
                  _const_spec((1, RW_COLS)), vec(), _const_spec(wa.shape), vec(),
                  _const_spec(g2.shape), vec(), vec(), vec(), vec(), vec(),
                  _const_spec(tril.shape)],
        out_specs=pl.BlockSpec((1, tt, RW_WIDTH), lambda i, j: (i, j, 0)),
        out_shape=jax.ShapeDtypeStruct((b, t, RW_WIDTH), BF16),
        scratch_shapes=[seq_scratch() for _ in range(7)]
        + [pltpu.VMEM((n_pairs, LANES, LANES), F32),
           pltpu.VMEM((RW_COLS // LANES, SUBLANES + tt, LANES), F32)],
        compiler_params=pltpu.CompilerParams(dimension_semantics=("arbitrary", "arbitrary"),
                                             vmem_limit_bytes=VMEM_LIMIT),
        name="rwkv7_mixer",
    )(p_rw, mu, w0, wa, a0, g2, k_k, k_a, r_k, lnw, lnb, tril)


def _proj_retention_kernel(tiles_per_row, x_ref, g_ref, wrw_ref, wret_ref, wg_ref, pos_ref,
                           prw_ref, pg_ref, o_ref, p_s, vg_s, q_s, k_s, o_s, dmask_s, state_s):
    tt = x_ref.shape[0]
    C = RET_CHUNK
    QW = RET_QK_WIDTH
    half = RET_QK_DIM // 2
    n_pairs = RET_HEADS // 2
    step = pl.program_id(0)

    @pl.when(step == 0)
    def _():
        p_s[...] = jnp.zeros_like(p_s)

    @pl.when((lax.rem(step, tiles_per_row) == 1 % tiles_per_row) | (step == 0))
    def _():
        state_s[...] = jnp.zeros_like(state_s)

    hn = _rms_norm(x_ref[...], g_ref[...]).astype(BF16)
    vg_s[...] = p_s[:, 2 * QW:]

    lane = lax.broadcasted_iota(jnp.int32, (1, LANES), 1)
    freq = (lane & (half - 1)).astype(F32)
    inv = jnp.exp(freq * (-math.log(ROPE_BASE) / half))
    first = (lane & (RET_QK_DIM - 1)) < half
    n_copies = LANES // half
    rp = tt // n_copies
    lane_blk = lane >> int(math.log2(half))
    pos = pos_ref[...].astype(F32)
    ang = pos[0:rp, :] * inv
    for g in range(1, n_copies):
        ang = jnp.where(lane_blk == g, pos[g * rp:(g + 1) * rp, :] * inv, ang)

    def spread(tab):
        rolled = [tab] + [pltpu.roll(tab, half * k, 1) for k in range(1, n_copies)]
        groups = []
        for g in range(n_copies):
            full = rolled[(0 - g) % n_copies]
            for blk in range(1, n_copies):
                full = jnp.where(lane_blk == blk, rolled[(blk - g) % n_copies], full)
            groups.append(full)
        return jnp.concatenate(groups, axis=0)

    cos = spread(jnp.cos(ang))
    sin = spread(jnp.sin(ang))
    sin = jnp.where(first, -sin, sin)

    def rope(z):
        swapped = jnp.where(first, pltpu.roll(z, LANES - half, 1), pltpu.roll(z, half, 1))
        return z * cos + swapped * sin

    for j in range(n_pairs):
        sl = slice(LANES * j, LANES * (j + 1))
        q_s[:, sl] = rope(p_s[:, sl].astype(F32))
        k_s[:, sl] = rope(p_s[:, QW + LANES * j:QW + LANES * (j + 1)].astype(F32)) * (RET_QK_DIM ** -0.5)

    prw_ref[...] = jnp.dot(hn, wrw_ref[...], preferred_element_type=F32).astype(BF16)

    lane2 = lax.broadcasted_iota(jnp.int32, (C, 2 * LANES), 1)
    trow2 = lax.broadcasted_iota(jnp.int32, (C, 2 * LANES), 0)
    lane1 = lax.broadcasted_iota(jnp.int32, (C, LANES), 1)
    trow1 = lax.broadcasted_iota(jnp.int32, (C, LANES), 0).astype(F32)
    m0 = lane1 < RET_QK_DIM
    head_of_lane = int(math.log2(LANES))
    head_of_qk = int(math.log2(RET_QK_DIM))
    q_decay, k_decay, chunk_decay = [], [], []
    for j in range(n_pairs):
        lg = lambda hsel: jnp.log1p(-jnp.exp2(-5.0 - (2 * j + hsel).astype(F32)))
        lg2 = lg(lane2 >> head_of_lane)
        rel = (trow2 - (lane2 & (LANES - 1))).astype(F32)
        dmask_s[j] = jnp.where(rel >= 0, jnp.exp(jnp.maximum(rel, 0.0) * lg2), 0.0)
        lg1 = lg(lane1 >> head_of_qk)
        q_decay.append(jnp.exp((trow1 + 1.0) * lg1))
        k_decay.append(jnp.exp((C - 1.0 - trow1) * lg1))
        srow = lax.broadcasted_iota(jnp.int32, (LANES, LANES), 0) >> head_of_qk
        chunk_decay.append(jnp.exp(C * lg(srow)))

    def bd(x):
        z = jnp.zeros_like(x)
        return jnp.concatenate([jnp.where(m0, x, z), jnp.where(m0, z, x)], axis=0)

    units = [(c, j) for c in range(tt // C) for j in range(n_pairs)]
    hrow = lax.broadcasted_iota(jnp.int32, (LANES, LANES), 0) < RET_QK_DIM
    vcols = lambda j: slice(2 * LANES * j, 2 * LANES * (j + 1))
    scores, incr, qd = {}, {}, {}
    for c, j in units:
        rows = slice(c * C, (c + 1) * C)
        sl = slice(LANES * j, LANES * (j + 1))
        qj = q_s[rows, sl]
        kj = k_s[rows, sl]
        scores[c, j] = (_dot_nt(qj, bd(kj.astype(BF16))) * dmask_s[j]).astype(BF16)
        qd[c, j] = (qj * q_decay[j]).astype(BF16)
        x = _dot_tn(kj * k_decay[j], vg_s[rows, vcols(j)])
        incr[c, j] = jnp.where(hrow, x[:, 0:LANES], x[:, LANES:2 * LANES])
    state = {}
    for j in range(n_pairs):
        r_cur = state_s[j]
        for c in range(tt // C):
            state[c, j] = r_cur.astype(BF16)
            r_cur = r_cur * chunk_decay[j] + incr[c, j]
        state_s[j] = r_cur
    pg_ref[...] = jnp.dot(hn, wg_ref[...], preferred_element_type=F32).astype(BF16)
    for c, j in units:
        rows = slice(c * C, (c + 1) * C)
        vpair = vg_s[rows, vcols(j)]
        zq = jnp.zeros_like(qd[c, j])
        for hh in range(2):
            h = 2 * j + hh
            qm = jnp.where(m0, qd[c, j], zq) if hh == 0 else jnp.where(m0, zq, qd[c, j])
            lhs = jnp.concatenate([scores[c, j][:, LANES * hh:LANES * (hh + 1)], qm], axis=1)
            rhs = jnp.concatenate([vpair[:, LANES * hh:LANES * (hh + 1)], state[c, j]], axis=0)
            o_s[rows, LANES * h:LANES * (h + 1)] = jnp.dot(lhs, rhs, preferred_element_type=F32)

    def project_ret(piece):
        cols = slice(piece * RET_COLS // 3, (piece + 1) * RET_COLS // 3)
        p_s[:, cols] = jnp.dot(hn, wret_ref[:, cols], preferred_element_type=F32).astype(BF16)

    blocks = [slice(c0, c0 + RET_V_DIM) for c0 in range(0, RET_V_WIDTH, RET_V_DIM)]
    project_ret(0)
    means = [jnp.mean(o_s[:, sl], axis=-1, keepdims=True) for sl in blocks]
    project_ret(1)
    centred = [o_s[:, sl] - m for sl, m in zip(blocks, means)]
    variances = [jnp.mean(oc * oc, axis=-1, keepdims=True) for oc in centred]
    project_ret(2)
    for sl, oc, var in zip(blocks, centred, variances):
        gate = vg_s[:, RET_V_WIDTH + sl.start:RET_V_WIDTH + sl.stop].astype(F32)
        o_ref[:, sl] = (gate * jax.nn.sigmoid(gate) * oc * lax.rsqrt(var + RET_GN_EPS)).astype(o_ref.dtype)


def _proj_retention(x2, g, w_rw, w_ret, w_gate, pos2, tt, tiles_per_row):
    n = x2.shape[0]
    n_tiles = n // tt
    n_pairs = RET_HEADS // 2
    cur = lambda w: pl.BlockSpec((tt, w), lambda i: (jnp.minimum(i, n_tiles - 1), 0))
    prev = lambda w: pl.BlockSpec((tt, w), lambda i: (jnp.maximum(i - 1, 0), 0))
    return pl.pallas_call(
        functools.partial(_proj_retention_kernel, tiles_per_row),
        grid=(n_tiles + 1,),
        in_specs=[cur(D_MODEL), _const_spec((1, D_MODEL)), _const_spec(w_rw.shape),
                  _const_spec(w_ret.shape), _const_spec(w_gate.shape), prev(1)],
        out_specs=[cur(RW_COLS), cur(2 * D_MODEL), prev(RET_V_WIDTH)],
        out_shape=[jax.ShapeDtypeStruct((n, RW_COLS), BF16),
                   jax.ShapeDtypeStruct((n, 2 * D_MODEL), BF16),
                   jax.ShapeDtypeStruct((n, RET_V_WIDTH), BF16)],
        scratch_shapes=[pltpu.VMEM((tt, RET_COLS), BF16), pltpu.VMEM((tt, 2 * RET_V_WIDTH), BF16),
                        pltpu.VMEM((tt, RET_QK_WIDTH), F32), pltpu.VMEM((tt, RET_QK_WIDTH), F32),
                        pltpu.VMEM((tt, RET_V_WIDTH), F32),
                        pltpu.VMEM((n_pairs, RET_CHUNK, 2 * LANES), F32),
                        pltpu.VMEM((n_pairs, LANES, LANES), F32)],
        compiler_params=pltpu.CompilerParams(dimension_semantics=("arbitrary",),
                                             vmem_limit_bytes=VMEM_LIMIT),
        name="projection_retention",
    )(x2, g, w_rw, w_ret, w_gate, pos2)


def _merge_kernel(x_ref, yrw_ref, yret_ref, pg_ref, wbrw_ref, wbret_ref, wout_ref,
                  npost_ref, nffn_ref, h_ref, hn_ref):
    tm = x_ref.shape[0]
    subs = [slice(r, r + tm // MERGE_SPLIT) for r in range(0, tm, tm // MERGE_SPLIT)]
    branches = [(jnp.dot(yrw_ref[rs, :], wbrw_ref[...], preferred_element_type=F32),
                 jnp.dot(yret_ref[rs, :], wbret_ref[...], preferred_element_type=F32)) for rs in subs]

    def finish(rs, mx):
        h = x_ref[rs, :] + _rms_norm(mx, npost_ref[...])
        h_ref[rs, :] = h
        hn_ref[rs, :] = _rms_norm(h, nffn_ref[...]).astype(BF16)

    pending = None
    for rs, (b_rw, b_ret) in zip(subs, branches):
        g_rw = jax.nn.sigmoid(pg_ref[rs, 0:D_MODEL].astype(F32))
        g_ret = jax.nn.sigmoid(pg_ref[rs, D_MODEL:2 * D_MODEL].astype(F32))
        mx = _dot(g_rw * b_rw + g_ret * b_ret, wout_ref[...])
        if pending is not None:
            finish(*pending)
        pending = (rs, mx)
    finish(*pending)


def _merge(x2, y_rw, y_ret, p_gate, wb_rw, wb_ret, w_out, n_post, n_ffn, tm):
    n = x2.shape[0]
    row = lambda w: pl.BlockSpec((tm, w), lambda i: (i, 0))
    return pl.pallas_call(
        _merge_kernel,
        grid=(n // tm,),
        in_specs=[row(D_MODEL), row(RW_WIDTH), row(RET_V_WIDTH), row(2 * D_MODEL),
                  _const_spec(wb_rw.shape), _const_spec(wb_ret.shape), _const_spec(w_out.shape),
                  _const_spec((1, D_MODEL)), _const_spec((1, D_MODEL))],
        out_specs=[row(D_MODEL), row(D_MODEL)],
        out_shape=[jax.ShapeDtypeStruct((n, D_MODEL), F32),
                   jax.ShapeDtypeStruct((n, D_MODEL), BF16)],
        compiler_params=pltpu.CompilerParams(dimension_semantics=("arbitrary",),
                                             vmem_limit_bytes=VMEM_LIMIT),
        name="merge_out_projection",
    )(x2, y_rw, y_ret, p_gate, wb_rw, wb_ret, w_out, n_post, n_ffn)


def _gelu_tanh(x):
    k0 = -2.0 * math.sqrt(2.0 / math.pi) * math.log2(math.e)
    k1 = k0 * 0.044715
    e = jnp.exp2(x * (k1 * (x * x) + k0))
    return x * (1.0 / (1.0 + e))


def _ffn_kernel(hn_ref, h_ref, wup_ref, cw_ref, cb_ref, wdn_ref, npost_ref, o_ref, slab_s, acc_s):
    tt = hn_ref.shape[1]
    fc = FF_CHUNK
    first_tile = pl.program_id(1) == 0

    hn = hn_ref[0]

    def conv(u, col):
        outs = []
        for blk in range(fc // LANES):
            c0 = col + blk * LANES
            slab = slab_s.at[c0 // LANES]
            prev = jnp.where(first_tile, 0.0, slab[tt:tt + SUBLANES, :])
            slab[0:SUBLANES, :] = prev
            ub = u[:, blk * LANES:(blk + 1) * LANES]
            slab[SUBLANES:SUBLANES + tt, :] = ub
            s1 = slab[SUBLANES - 1:SUBLANES - 1 + tt, :]
            s2 = slab[SUBLANES - 2:SUBLANES - 2 + tt, :]
            cw = cw_ref[:, c0:c0 + LANES]
            outs.append(ub * cw[2:3, :] + s1 * cw[1:2, :] + s2 * cw[0:1, :] + cb_ref[:, c0:c0 + LANES])
        return jnp.concatenate(outs, axis=1)

    def up(j):
        cg = j * fc
        cv = D_FF + j * fc
        return (jnp.dot(hn, wup_ref[:, cg:cg + fc], preferred_element_type=F32),
                jnp.dot(hn, wup_ref[:, cv:cv + fc], preferred_element_type=F32))

    n_chunks = D_FF // fc
    u_next = up(0)
    acts = []
    for j in range(n_chunks):
        cg = j * fc
        cv = D_FF + j * fc
        u_gate, u_val = u_next
        if j + 1 < n_chunks:
            u_next = up(j + 1)
        gate = conv(u_gate, cg)
        val = conv(u_val, cv)
        acts.append((_gelu_tanh(gate) * val).astype(BF16))
        if len(acts) == FF_DOWN_GROUP or j == n_chunks - 1:
            k0 = (j + 1 - len(acts)) * fc
            act = acts[0] if len(acts) == 1 else jnp.concatenate(acts, axis=1)
            part = jnp.dot(act, wdn_ref[k0:k0 + len(acts) * fc, :], preferred_element_type=F32)
            if k0 == 0:
                acc_s[...] = part
            else:
                acc_s[...] += part
            acts = []

    o_ref[0] = h_ref[0] + _rms_norm(acc_s[...], npost_ref[...])


def _conv_ffn(hn3, h3, w_up, conv_w, conv_b, w_down, n_post, tt):
    b, t, _ = hn3.shape
    tile = lambda: pl.BlockSpec((1, tt, D_MODEL), lambda i, j: (i, j, 0))
    return pl.pallas_call(
        _ffn_kernel,
        grid=(b, t // tt),
        in_specs=[tile(), tile(), _const_spec(w_up.shape), _const_spec(conv_w.shape),
                  _const_spec(conv_b.shape), _const_spec(w_down.shape), _const_spec((1, D_MODEL))],
        out_specs=tile(),
        out_shape=jax.ShapeDtypeStruct((b, t, D_MODEL), F32),
        scratch_shapes=[pltpu.VMEM((2 * D_FF // LANES, SUBLANES + tt, LANES), F32),
                        pltpu.VMEM((tt, D_MODEL), F32)],
        compiler_params=pltpu.CompilerParams(dimension_semantics=("arbitrary", "arbitrary"),
                                             vmem_limit_bytes=VMEM_LIMIT),
        name="conv_ffn",
    )(hn3, h3, w_up, conv_w, conv_b, w_down, n_post)


def kernel(x, positions, norm_mix_pre, norm_mix_post, norm_ffn_pre, norm_ffn_post, w_in, rw_mu, rw_w0, rw_w2, rw_a0, rw_a2, rw_g2, rw_k_k, rw_k_a, rw_r_k, rw_lnx_w, rw_lnx_b, w_branch_rw, w_branch_ret, w_out, ffn_w_up, ffn_conv_w, ffn_conv_b, ffn_w_down):
    b, t, d = x.shape
    assert d == D_MODEL and norm_mix_pre.shape[0] == 1
    n = b * t
    tm = min(TOKEN_TILE, n)
    tt = min(SEQ_TILE, t)
    assert n % tm == 0 and t % tt == 0 and tt % RET_CHUNK == 0 and t % min(RW_SEQ_TILE, t) == 0
    assert ffn_conv_w.shape[1] == CONV_WIDTH == 3
    vec = lambda a: a[0].reshape(1, -1).astype(F32)

    w_in_b = w_in[0].astype(BF16)
    w_rw = w_in_b[:, 0:RW_COLS]
    w_ret = w_in_b[:, RW_COLS:RW_COLS + RET_COLS]
    w_gate = w_in_b[:, RW_COLS + RET_COLS:]
    zeros = jnp.zeros((RW_DECAY_RANK, RW_WIDTH), F32)
    wa = jnp.concatenate([jnp.concatenate([rw_w2[0], zeros], axis=1),
                          jnp.concatenate([zeros, rw_a2[0]], axis=1)], axis=0).astype(BF16)

    x2 = x.reshape(n, d)
    p_rw, p_gate, y_ret = _proj_retention(x2, vec(norm_mix_pre), w_rw, w_ret, w_gate,
                                          positions.reshape(n, 1), tt, t // tt)

    y_rw = _rwkv_mixer(p_rw.reshape(b, t, RW_COLS), vec(rw_mu), vec(rw_w0), wa, vec(rw_a0),
                       rw_g2[0].astype(BF16), vec(rw_k_k), vec(rw_k_a), vec(rw_r_k),
                       vec(rw_lnx_w), vec(rw_lnx_b), min(RW_SEQ_TILE, t))

    h, hn = _merge(x2, y_rw.reshape(n, RW_WIDTH), y_ret, p_gate,
                   w_branch_rw[0].astype(BF16), w_branch_ret[0].astype(BF16), w_out[0].astype(BF16),
                   vec(norm_mix_post), vec(norm_ffn_pre), tm)

    out = _conv_ffn(hn.reshape(b, t, d), h.reshape(b, t, d), ffn_w_up[0].astype(BF16),
                    ffn_conv_w[0].astype(F32), vec(ffn_conv_b), ffn_w_down[0].astype(BF16),
                    vec(norm_ffn_post), tt)
    return out.astype(x.dtype)
```

```python
import functools
import math

import numpy as np
import jax
import jax.numpy as jnp
from jax import lax
from jax.experimental import pallas as pl
from jax.experimental.pallas import tpu as pltpu

F32 = jnp.float32
BF16 = jnp.bfloat16

D_MODEL = 1024
RW_HEAD_DIM = 64
RW_WIDTH = 512
RW_HEADS = 8
RW_DECAY_RANK = 64
RW_ICLR_RANK = 64
RW_GATE_RANK = 128
RW_COLS = 3 * RW_WIDTH + RW_DECAY_RANK + RW_ICLR_RANK + RW_GATE_RANK
RW_GN_EPS = 64e-5
RW_CHUNK = 64
RW_GROUP = 16
RW_CUM_CHUNKS = 2
RET_QK_DIM = 64
RET_QK_WIDTH = 512
RET_HEADS = 8
RET_V_DIM = 128
RET_V_WIDTH = 1024
RET_COLS = 2 * RET_QK_WIDTH + 2 * RET_V_WIDTH
RET_CHUNK = 128
ROPE_BASE = 10000.0
RET_GN_EPS = 1e-5
D_FF = 2816
CONV_WIDTH = 3
RMS_EPS = 1e-6

LANES = 128
SUBLANES = 8
MXU_DIM = 256
VMEM_LIMIT = 56 * 1024 * 1024
TOKEN_TILE = 1024
SEQ_TILE = 512
RW_SEQ_TILE = 1024
MERGE_SPLIT = 8
FF_CHUNK = 256
FF_DOWN_GROUP = 11


def _dot(a, b):
    return jnp.dot(a.astype(BF16), b.astype(BF16), preferred_element_type=F32)


def _dot_nt(a, b):
    return lax.dot_general(a.astype(BF16), b.astype(BF16), (((1,), (1,)), ((), ())),
                           preferred_element_type=F32)


def _dot_tn(a, b):
    return lax.dot_general(a.astype(BF16), b.astype(BF16), (((0,), (0,)), ((), ())),
                           preferred_element_type=F32)


def _group_sum(x, m):
    xb = x.astype(BF16)
    bw = m.shape[0]
    parts = [jnp.dot(xb[:, c:c + bw], m, preferred_element_type=F32) for c in range(0, x.shape[1], bw)]
    return parts[0] if len(parts) == 1 else jnp.concatenate(parts, axis=1)


def _rms_norm(x, g):
    ms = jnp.mean(x * x, axis=-1, keepdims=True)
    return x * lax.rsqrt(ms + RMS_EPS) * g


def _const_spec(shape):
    nd = len(shape)
    return pl.BlockSpec(shape, lambda *_: (0,) * nd, pipeline_mode=pl.Buffered(1))


def _rwkv_kernel(p_ref, mu_ref, w0_ref, wa_ref, a0_ref, g2_ref, kk_ref, ka_ref, rk_ref,
                 lnw_ref, lnb_ref, gsum_ref, tril_ref, o_ref,
                 r_s, k_s, v_s, a_s, b_s, lw_s, y_s, state_s, slab_s):
    tt = p_ref.shape[1]
    W = RW_WIDTH
    C = RW_CHUNK

    @pl.when(pl.program_id(1) == 0)
    def _():
        state_s[...] = jnp.zeros_like(state_s)

    first_tile = pl.program_id(1) == 0
    blocks = []
    for blk in range(RW_COLS // LANES):
        sl = slice(blk * LANES, (blk + 1) * LANES)
        slab = slab_s.at[blk]
        slab[0:SUBLANES, :] = jnp.where(first_tile, 0.0, slab[tt:tt + SUBLANES, :])
        pb = p_ref[0, :, sl].astype(F32)
        slab[SUBLANES:SUBLANES + tt, :] = pb
        blocks.append(pb + (slab[SUBLANES - 1:SUBLANES - 1 + tt, :] - pb) * mu_ref[:, sl])
    p = jnp.concatenate(blocks, axis=1)

    r = p[:, 0:W]
    k = p[:, W:2 * W]
    v = p[:, 2 * W:3 * W]
    lane = lax.broadcasted_iota(jnp.int32, (1, LANES), 1)
    wa_in = p[:, 3 * W:3 * W + LANES]
    wa_in = jnp.where(lane < RW_DECAY_RANK, jnp.tanh(wa_in), wa_in)
    wa = _dot(wa_in, wa_ref[...])
    gd = p[:, 3 * W + LANES:3 * W + 2 * LANES]

    lw_s[...] = -math.exp(-0.5) * jax.nn.sigmoid(w0_ref[...] + wa[:, 0:W])
    a_lr = jax.nn.sigmoid(a0_ref[...] + wa[:, W:2 * W])

    gsum = gsum_ref[...]
    kk = k * kk_ref[...]
    kk = kk * lax.rsqrt(jnp.maximum(_group_sum(kk * kk, gsum), 1e-24))
    k = k * (1.0 + (a_lr - 1.0) * ka_ref[...])

    r_s[...] = r
    k_s[...] = k
    v_s[...] = v
    a_s[...] = -kk
    b_s[...] = kk * a_lr

    lane2 = lax.broadcasted_iota(jnp.int32, (C, LANES), 1)
    trow = lax.broadcasted_iota(jnp.int32, (C, LANES), 0)
    m0 = lane2 < RW_HEAD_DIM
    scol = jnp.where(m0, lane2, lane2 - RW_HEAD_DIM)
    strict = trow > scol
    incl = trow >= scol
    brow = lax.broadcasted_iota(jnp.int32, (LANES, LANES), 0) < RW_HEAD_DIM
    bcol = lax.broadcasted_iota(jnp.int32, (LANES, LANES), 1) < RW_HEAD_DIM
    blockmask = brow == bcol
    tril = tril_ref[...]

    def bd(x):
        z = jnp.zeros_like(x)
        return jnp.concatenate([jnp.where(m0, x, z), jnp.where(m0, z, x)], axis=0)

    n_pairs = RW_HEADS // 2
    grp = min(RW_GROUP, tt // C)
    gc = grp * C
    grow = lax.broadcasted_iota(jnp.int32, (gc, 1), 0)
    n_levels = int(math.log2(C))

    def prep_body(i, carry):
        rows = pl.ds(pl.multiple_of(i * gc, gc), gc)
        lw = lw_s[rows, :]
        h1 = lw.astype(BF16)
        h2 = (lw - h1.astype(F32)).astype(BF16)
        cb = tril.shape[0]
        cum = jnp.concatenate(
            [jnp.dot(tril, h1[q * cb:(q + 1) * cb], preferred_element_type=F32)
             + jnp.dot(tril, h2[q * cb:(q + 1) * cb], preferred_element_type=F32)
             for q in range(gc // cb)], axis=0)
        tot = cum[C - 1:C, :]
        for g in range(1, grp):
            tot = jnp.where(grow >= g * C, cum[(g + 1) * C - 1:(g + 1) * C, :], tot)
        w_inv = jnp.exp(-cum)
        w_rem = jnp.exp(tot - cum)
        rc = r_s[rows, :]
        kc = k_s[rows, :]
        bc = b_s[rows, :]
        v32c = v_s[rows, :]
        vc = v32c.astype(BF16)
        rt32 = rc * jnp.exp(cum)
        at32 = a_s[rows, :] * jnp.exp(cum - lw)
        rt = rt32.astype(BF16)
        at = at32.astype(BF16)
        bt = (bc * w_inv).astype(BF16)
        kt = (kc * w_inv).astype(BF16)
        bh = (bc * w_rem).astype(BF16)
        kh = (kc * w_rem).astype(BF16)

        units = [(g, j) for g in range(grp) for j in range(n_pairs)]
        blk = lambda arr, g, j: arr[g * C:(g + 1) * C, LANES * j:LANES * (j + 1)]
        zero = jnp.zeros((C, LANES), F32)
        stack2 = lambda x: jnp.concatenate([x, x], axis=0)
        zero2 = jnp.zeros((2 * C, LANES), F32)
        a_rb, a_rk, z, n_bd = {}, {}, {}, {}
        ak_bd, vw, a_st = {}, {}, {}
        for u in units:
            lhs = jnp.concatenate([blk(at, *u), blk(rt, *u)], axis=0)
            gram = _dot_nt(lhs, jnp.concatenate([bd(blk(bt, *u)), bd(blk(kt, *u))], axis=0))
            a_ab = jnp.where(strict, gram[0:C, 0:LANES], zero)
            a_ak = jnp.where(strict, gram[0:C, LANES:2 * LANES], zero)
            n_bd[u] = jnp.where(blockmask, stack2(a_ab), zero2).astype(BF16)
            ak_bd[u] = jnp.where(blockmask, stack2(a_ak), zero2).astype(BF16)
            a_rb[u] = jnp.where(incl, gram[C:2 * C, 0:LANES], zero).astype(BF16)
            a_rk[u] = jnp.where(incl, gram[C:2 * C, LANES:2 * LANES], zero).astype(BF16)
            v32 = blk(v32c, *u)
            vw[u] = jnp.concatenate([pltpu.roll(v32, RW_HEAD_DIM, 1), v32], axis=0).astype(BF16)
            a32 = blk(at32, *u)
            a_st[u] = jnp.concatenate([a32, pltpu.roll(a32, RW_HEAD_DIM, 1)], axis=0)
        for u in units:
            akv = jnp.dot(ak_bd[u], vw[u], preferred_element_type=F32)
            z[u] = jnp.where(bcol, a_st[u], akv)
        for lvl in range(n_levels):
            last = lvl == n_levels - 1
            for u in units:
                zb = z[u].astype(BF16)
                rhs = zb if last else jnp.concatenate([zb, n_bd[u]], axis=1)
                comb = jnp.dot(n_bd[u], rhs, preferred_element_type=F32)
                z[u] = z[u] + comb[:, 0:LANES]
                if not last:
                    n_bd[u] = comb[:, LANES:2 * LANES].astype(BF16)

        def finish(u):
            top = z[u][0:C]
            bot = z[u][C:2 * C]
            apb = jnp.where(m0, top, pltpu.roll(bot, RW_HEAD_DIM, 1)).astype(BF16)
            uvb = jnp.where(m0, pltpu.roll(top, RW_HEAD_DIM, 1), bot).astype(BF16)
            out = jnp.dot(a_rb[u], jnp.concatenate([bd(apb), bd(uvb)], axis=1),
                          preferred_element_type=F32)
            rp = (blk(rt32, *u) + out[:, 0:LANES]).astype(BF16)
            yv = out[:, LANES:2 * LANES] + jnp.dot(a_rk[u], bd(blk(vc, *u)), preferred_element_type=F32)
            x = _dot_tn(apb, blk(bh, *u))
            x = jnp.where(blockmask, x, jnp.zeros_like(x)).astype(BF16)
            e = _dot_tn(jnp.concatenate([uvb, blk(vc, *u)], axis=0),
                        jnp.concatenate([blk(bh, *u), blk(kh, *u)], axis=0))
            e = jnp.where(blockmask, e, jnp.zeros_like(e))
            return rp, yv, x, e

        s_cur = [state_s[j] for j in range(n_pairs)]

        def scan_step(g, fin):
            crow = pl.ds(pl.multiple_of(i * gc + g * C, C), C)
            w_tot = jnp.exp(cum[(g + 1) * C - 1:(g + 1) * C, :])
            for j in range(n_pairs):
                sl = slice(LANES * j, LANES * (j + 1))
                rp, yv, x, e = fin[j]
                sb = s_cur[j].astype(BF16)
                y_s[crow, sl] = _dot_nt(rp, sb) + yv
                s_cur[j] = s_cur[j] * w_tot[:, sl] + e + jnp.dot(sb, x, preferred_element_type=F32)

        pending = None
        for g in range(grp):
            fin = [finish((g, j)) for j in range(n_pairs)]
            if pending is not None:
                scan_step(*pending)
            pending = (g, fin)
        scan_step(*pending)
        for j in range(n_pairs):
            state_s[j] = s_cur[j]
        return carry

    lax.fori_loop(0, tt // gc, prep_body, 0)

    g = _dot(jax.nn.sigmoid(gd), g2_ref[...])
    bonus = _group_sum(r_s[...] * k_s[...] * rk_ref[...], gsum) * v_s[...]
    y = y_s[...]
    inv_n = 1.0 / RW_HEAD_DIM
    mean = _group_sum(y, gsum) * inv_n
    yc = y - mean
    var = _group_sum(yc * yc, gsum) * inv_n
    yn = yc * lax.rsqrt(var + RW_GN_EPS) * lnw_ref[...] + lnb_ref[...]
    o_ref[0] = ((yn + bonus) * g).astype(o_ref.dtype)


def _rwkv_mixer(p_rw, mu, w0, wa, a0, g2, k_k, k_a, r_k, lnw, lnb, tt):
    b, t, _ = p_rw.shape
    gsum = jnp.asarray(np.kron(np.eye(MXU_DIM // RW_HEAD_DIM), np.ones((RW_HEAD_DIM, RW_HEAD_DIM))), BF16)
    tril = jnp.asarray(np.kron(np.eye(RW_CUM_CHUNKS), np.tril(np.ones((RW_CHUNK, RW_CHUNK)))), BF16)
    n_pairs = RW_HEADS // 2
    vec = lambda: _const_spec((1, RW_WIDTH))
    seq_scratch = lambda: pltpu.VMEM((tt, RW_WIDTH), F32)
    return pl.pallas_call(
        _rwkv_kernel,
        grid=(b, t // tt),
        in_specs=[pl.BlockSpec((1, tt, RW_COLS), lambda i, j: (i, j, 0)),
                  _const_spec((1, RW_COLS)), vec(), _const_spec(wa.shape), vec(),
                  _const_spec(g2.shape), vec(), vec(), vec(), vec(), vec(),
                  _const_spec(gsum.shape), _const_spec(tril.shape)],
        out_specs=pl.BlockSpec((1, tt, RW_WIDTH), lambda i, j: (i, j, 0)),
        out_shape=jax.ShapeDtypeStruct((b, t, RW_WIDTH), BF16),
        scratch_shapes=[seq_scratch() for _ in range(7)]
        + [pltpu.VMEM((n_pairs, LANES, LANES), F32),
           pltpu.VMEM((RW_COLS // LANES, SUBLANES + tt, LANES), F32)],
        compiler_params=pltpu.CompilerParams(dimension_semantics=("arbitrary", "arbitrary"),
                                             vmem_limit_bytes=VMEM_LIMIT),
        name="rwkv7_mixer",
    )(p_rw, mu, w0, wa, a0, g2, k_k, k_a, r_k, lnw, lnb, gsum, tril)


def _proj_retention_kernel(tiles_per_row, x_ref, g_ref, win_ref, pos_ref,
                           prw_ref, pg_ref, o_ref, p_s, vg_s, q_s, k_s, o_s, dmask_s, state_s):
    tt = x_ref.shape[0]
    C = RET_CHUNK
    QW = RET_QK_WIDTH
    half = RET_QK_DIM // 2
    n_pairs = RET_HEADS // 2
    step = pl.program_id(0)

    @pl.when(step == 0)
    def _():
        p_s[...] = jnp.zeros_like(p_s)

    @pl.when((lax.rem(step, tiles_per_row) == 1 % tiles_per_row) | (step == 0))
    def _():
        state_s[...] = jnp.zeros_like(state_s)

    hn = _rms_norm(x_ref[...], g_ref[...]).astype(BF16)
    vg_s[...] = p_s[:, 2 * QW:]

    lane = lax.broadcasted_iota(jnp.int32, (1, LANES), 1)
    freq = (lane & (half - 1)).astype(F32)
    inv = jnp.exp(freq * (-math.log(ROPE_BASE) / half))
    first = (lane & (RET_QK_DIM - 1)) < half
    n_copies = LANES // half
    rp = tt // n_copies
    lane_blk = lane >> int(math.log2(half))
    pos = pos_ref[...].astype(F32)
    ang = pos[0:rp, :] * inv
    for g in range(1, n_copies):
        ang = jnp.where(lane_blk == g, pos[g * rp:(g + 1) * rp, :] * inv, ang)

    def spread(tab):
        rolled = [tab] + [pltpu.roll(tab, half * k, 1) for k in range(1, n_copies)]
        groups = []
        for g in range(n_copies):
            full = rolled[(0 - g) % n_copies]
            for blk in range(1, n_copies):
                full = jnp.where(lane_blk == blk, rolled[(blk - g) % n_copies], full)
            groups.append(full)
        return jnp.concatenate(groups, axis=0)

    cos = spread(jnp.cos(ang))
    sin = spread(jnp.sin(ang))
    sin = jnp.where(first, -sin, sin)

    def rope(z):
        swapped = jnp.where(first, pltpu.roll(z, LANES - half, 1), pltpu.roll(z, half, 1))
        return z * cos + swapped * sin

    for j in range(n_pairs):
        sl = slice(LANES * j, LANES * (j + 1))
        q_s[:, sl] = rope(p_s[:, sl].astype(F32))
        k_s[:, sl] = rope(p_s[:, QW + LANES * j:QW + LANES * (j + 1)].astype(F32)) * (RET_QK_DIM ** -0.5)

    ret0 = RW_COLS
    gate0 = RW_COLS + RET_COLS
    prw_ref[...] = jnp.dot(hn, win_ref[:, 0:RW_COLS], preferred_element_type=F32).astype(BF16)

    lane2 = lax.broadcasted_iota(jnp.int32, (C, 2 * LANES), 1)
    trow2 = lax.broadcasted_iota(jnp.int32, (C, 2 * LANES), 0)
    lane1 = lax.broadcasted_iota(jnp.int32, (C, LANES), 1)
    trow1 = lax.broadcasted_iota(jnp.int32, (C, LANES), 0).astype(F32)
    m0 = lane1 < RET_QK_DIM
    head_of_lane = int(math.log2(LANES))
    head_of_qk = int(math.log2(RET_QK_DIM))
    q_decay, k_decay, chunk_decay = [], [], []
    for j in range(n_pairs):
        lg = lambda hsel: jnp.log1p(-jnp.exp2(-5.0 - (2 * j + hsel).astype(F32)))
        lg2 = lg(lane2 >> head_of_lane)
        rel = (trow2 - (lane2 & (LANES - 1))).astype(F32)
        dmask_s[j] = jnp.where(rel >= 0, jnp.exp(jnp.maximum(rel, 0.0) * lg2), 0.0)
        lg1 = lg(lane1 >> head_of_qk)
        q_decay.append(jnp.exp((trow1 + 1.0) * lg1))
        k_decay.append(jnp.exp((C - 1.0 - trow1) * lg1))
        srow = lax.broadcasted_iota(jnp.int32, (LANES, LANES), 0) >> head_of_qk
        chunk_decay.append(jnp.exp(C * lg(srow)))

    def bd(x):
        z = jnp.zeros_like(x)
        return jnp.concatenate([jnp.where(m0, x, z), jnp.where(m0, z, x)], axis=0)

    units = [(c, j) for c in range(tt // C) for j in range(n_pairs)]
    hrow = lax.broadcasted_iota(jnp.int32, (LANES, LANES), 0) < RET_QK_DIM
    vcols = lambda j: slice(2 * LANES * j, 2 * LANES * (j + 1))
    scores, incr, qd = {}, {}, {}
    for c, j in units:
        rows = slice(c * C, (c + 1) * C)
        sl = slice(LANES * j, LANES * (j + 1))
        qj = q_s[rows, sl]
        kj = k_s[rows, sl]
        scores[c, j] = (_dot_nt(qj, bd(kj.astype(BF16))) * dmask_s[j]).astype(BF16)
        qd[c, j] = (qj * q_decay[j]).astype(BF16)
        x = _dot_tn(kj * k_decay[j], vg_s[rows, vcols(j)])
        incr[c, j] = jnp.where(hrow, x[:, 0:LANES], x[:, LANES:2 * LANES])
    state = {}
    for j in range(n_pairs):
        r_cur = state_s[j]
        for c in range(tt // C):
            state[c, j] = r_cur.astype(BF16)
            r_cur = r_cur * chunk_decay[j] + incr[c, j]
        state_s[j] = r_cur
    pg_ref[...] = jnp.dot(hn, win_ref[:, gate0:gate0 + 2 * D_MODEL],
                          preferred_element_type=F32).astype(BF16)
    for c, j in units:
        rows = slice(c * C, (c + 1) * C)
        vpair = vg_s[rows, vcols(j)]
        zq = jnp.zeros_like(qd[c, j])
        for hh in range(2):
            h = 2 * j + hh
            qm = jnp.where(m0, qd[c, j], zq) if hh == 0 else jnp.where(m0, zq, qd[c, j])
            lhs = jnp.concatenate([scores[c, j][:, LANES * hh:LANES * (hh + 1)], qm], axis=1)
            rhs = jnp.concatenate([vpair[:, LANES * hh:LANES * (hh + 1)], state[c, j]], axis=0)
            o_s[rows, LANES * h:LANES * (h + 1)] = jnp.dot(lhs, rhs, preferred_element_type=F32)

    def project_ret(piece):
        cols = slice(piece * RET_COLS // 3, (piece + 1) * RET_COLS // 3)
        wcols = slice(ret0 + cols.start, ret0 + cols.stop)
        p_s[:, cols] = jnp.dot(hn, win_ref[:, wcols], preferred_element_type=F32).astype(BF16)

    blocks = [slice(c0, c0 + RET_V_DIM) for c0 in range(0, RET_V_WIDTH, RET_V_DIM)]
    project_ret(0)
    means = [jnp.mean(o_s[:, sl], axis=-1, keepdims=True) for sl in blocks]
    project_ret(1)
    centred = [o_s[:, sl] - m for sl, m in zip(blocks, means)]
    variances = [jnp.mean(oc * oc, axis=-1, keepdims=True) for oc in centred]
    project_ret(2)
    for sl, oc, var in zip(blocks, centred, variances):
        gate = vg_s[:, RET_V_WIDTH + sl.start:RET_V_WIDTH + sl.stop].astype(F32)
        o_ref[:, sl] = (gate * jax.nn.sigmoid(gate) * oc * lax.rsqrt(var + RET_GN_EPS)).astype(o_ref.dtype)


def _proj_retention(x2, g, w_in, pos2, tt, tiles_per_row):
    n = x2.shape[0]
    n_tiles = n // tt
    n_pairs = RET_HEADS // 2
    cur = lambda w: pl.BlockSpec((tt, w), lambda i: (jnp.minimum(i, n_tiles - 1), 0))
    prev = lambda w: pl.BlockSpec((tt, w), lambda i: (jnp.maximum(i - 1, 0), 0))
    return pl.pallas_call(
        functools.partial(_proj_retention_kernel, tiles_per_row),
        grid=(n_tiles + 1,),
        in_specs=[cur(D_MODEL), _const_spec((1, D_MODEL)), _const_spec(w_in.shape), prev(1)],
        out_specs=[cur(RW_COLS), cur(2 * D_MODEL), prev(RET_V_WIDTH)],
        out_shape=[jax.ShapeDtypeStruct((n, RW_COLS), BF16),
                   jax.ShapeDtypeStruct((n, 2 * D_MODEL), BF16),
                   jax.ShapeDtypeStruct((n, RET_V_WIDTH), BF16)],
        scratch_shapes=[pltpu.VMEM((tt, RET_COLS), BF16), pltpu.VMEM((tt, 2 * RET_V_WIDTH), BF16),
                        pltpu.VMEM((tt, RET_QK_WIDTH), F32), pltpu.VMEM((tt, RET_QK_WIDTH), F32),
                        pltpu.VMEM((tt, RET_V_WIDTH), F32),
                        pltpu.VMEM((n_pairs, RET_CHUNK, 2 * LANES), F32),
                        pltpu.VMEM((n_pairs, LANES, LANES), F32)],
        compiler_params=pltpu.CompilerParams(dimension_semantics=("arbitrary",),
                                             vmem_limit_bytes=VMEM_LIMIT),
        name="projection_retention",
    )(x2, g, w_in, pos2)


def _merge_kernel(x_ref, yrw_ref, yret_ref, pg_ref, wbrw_ref, wbret_ref, wout_ref,
                  npost_ref, nffn_ref, h_ref, hn_ref):
    tm = x_ref.shape[0]
    subs = [slice(r, r + tm // MERGE_SPLIT) for r in range(0, tm, tm // MERGE_SPLIT)]
    branches = [(jnp.dot(yrw_ref[rs, :], wbrw_ref[...], preferred_element_type=F32),
                 jnp.dot(yret_ref[rs, :], wbret_ref[...], preferred_element_type=F32)) for rs in subs]

    def finish(rs, mx):
        h = x_ref[rs, :] + _rms_norm(mx, npost_ref[...])
        h_ref[rs, :] = h
        hn_ref[rs, :] = _rms_norm(h, nffn_ref[...]).astype(BF16)

    pending = None
    for rs, (b_rw, b_ret) in zip(subs, branches):
        g_rw = jax.nn.sigmoid(pg_ref[rs, 0:D_MODEL].astype(F32))
        g_ret = jax.nn.sigmoid(pg_ref[rs, D_MODEL:2 * D_MODEL].astype(F32))
        mx = _dot(g_rw * b_rw + g_ret * b_ret, wout_ref[...])
        if pending is not None:
            finish(*pending)
        pending = (rs, mx)
    finish(*pending)


def _merge(x2, y_rw, y_ret, p_gate, wb_rw, wb_ret, w_out, n_post, n_ffn, tm):
    n = x2.shape[0]
    row = lambda w: pl.BlockSpec((tm, w), lambda i: (i, 0))
    return pl.pallas_call(
        _merge_kernel,
        grid=(n // tm,),
        in_specs=[row(D_MODEL), row(RW_WIDTH), row(RET_V_WIDTH), row(2 * D_MODEL),
                  _const_spec(wb_rw.shape), _const_spec(wb_ret.shape), _const_spec(w_out.shape),
                  _const_spec((1, D_MODEL)), _const_spec((1, D_MODEL))],
        out_specs=[row(D_MODEL), row(D_MODEL)],
        out_shape=[jax.ShapeDtypeStruct((n, D_MODEL), F32),
                   jax.ShapeDtypeStruct((n, D_MODEL), BF16)],
        compiler_params=pltpu.CompilerParams(dimension_semantics=("arbitrary",),
                                             vmem_limit_bytes=VMEM_LIMIT),
        name="merge_out_projection",
    )(x2, y_rw, y_ret, p_gate, wb_rw, wb_ret, w_out, n_post, n_ffn)


def _gelu_tanh(x):
    k0 = -2.0 * math.sqrt(2.0 / math.pi) * math.log2(math.e)
    k1 = k0 * 0.044715
    e = jnp.exp2(x * (k1 * (x * x) + k0))
    return x * (1.0 / (1.0 + e))


def _ffn_kernel(hn_ref, h_ref, wup_ref, cw_ref, cb_ref, wdn_ref, npost_ref, o_ref, slab_s, acc_s):
    tt = hn_ref.shape[1]
    fc = FF_CHUNK
    first_tile = pl.program_id(1) == 0

    hn = hn_ref[0]

    def conv(u, col):
        outs = []
        for blk in range(fc // LANES):
            c0 = col + blk * LANES
            slab = slab_s.at[c0 // LANES]
            prev = jnp.where(first_tile, 0.0, slab[tt:tt + SUBLANES, :])
            slab[0:SUBLANES, :] = prev
            ub = u[:, blk * LANES:(blk + 1) * LANES]
            slab[SUBLANES:SUBLANES + tt, :] = ub
            s1 = slab[SUBLANES - 1:SUBLANES - 1 + tt, :]
            s2 = slab[SUBLANES - 2:SUBLANES - 2 + tt, :]
            cw = cw_ref[:, c0:c0 + LANES]
            outs.append(ub * cw[2:3, :] + s1 * cw[1:2, :] + s2 * cw[0:1, :] + cb_ref[:, c0:c0 + LANES])
        return jnp.concatenate(outs, axis=1)

    def up(j):
        cg = j * fc
        cv = D_FF + j * fc
        return (jnp.dot(hn, wup_ref[:, cg:cg + fc], preferred_element_type=F32),
                jnp.dot(hn, wup_ref[:, cv:cv + fc], preferred_element_type=F32))

    n_chunks = D_FF // fc
    u_next = up(0)
    acts = []
    for j in range(n_chunks):
        cg = j * fc
        cv = D_FF + j * fc
        u_gate, u_val = u_next
        if j + 1 < n_chunks:
            u_next = up(j + 1)
        gate = conv(u_gate, cg)
        val = conv(u_val, cv)
        acts.append((_gelu_tanh(gate) * val).astype(BF16))
        if len(acts) == FF_DOWN_GROUP or j == n_chunks - 1:
            k0 = (j + 1 - len(acts)) * fc
            act = acts[0] if len(acts) == 1 else jnp.concatenate(acts, axis=1)
            part = jnp.dot(act, wdn_ref[k0:k0 + len(acts) * fc, :], preferred_element_type=F32)
            if k0 == 0:
                acc_s[...] = part
            else:
                acc_s[...] += part
            acts = []

    o_ref[0] = h_ref[0] + _rms_norm(acc_s[...], npost_ref[...])


def _conv_ffn(hn3, h3, w_up, conv_w, conv_b, w_down, n_post, tt):
    b, t, _ = hn3.shape
    tile = lambda: pl.BlockSpec((1, tt, D_MODEL), lambda i, j: (i, j, 0))
    return pl.pallas_call(
        _ffn_kernel,
        grid=(b, t // tt),
        in_specs=[tile(), tile(), _const_spec(w_up.shape), _const_spec(conv_w.shape),
                  _const_spec(conv_b.shape), _const_spec(w_down.shape), _const_spec((1, D_MODEL))],
        out_specs=tile(),
        out_shape=jax.ShapeDtypeStruct((b, t, D_MODEL), F32),
        scratch_shapes=[pltpu.VMEM((2 * D_FF // LANES, SUBLANES + tt, LANES), F32),
                        pltpu.VMEM((tt, D_MODEL), F32)],
        compiler_params=pltpu.CompilerParams(dimension_semantics=("arbitrary", "arbitrary"),
                                             vmem_limit_bytes=VMEM_LIMIT),
        name="conv_ffn",
    )(hn3, h3, w_up, conv_w, conv_b, w_down, n_post)


def kernel(x, positions, norm_mix_pre, norm_mix_post, norm_ffn_pre, norm_ffn_post, w_in, rw_mu, rw_w0, rw_w2, rw_a0, rw_a2, rw_g2, rw_k_k, rw_k_a, rw_r_k, rw_lnx_w, rw_lnx_b, w_branch_rw, w_branch_ret, w_out, ffn_w_up, ffn_conv_w, ffn_conv_b, ffn_w_down):
    b, t, d = x.shape
    assert d == D_MODEL and norm_mix_pre.shape[0] == 1
    n = b * t
    tm = min(TOKEN_TILE, n)
    tt = min(SEQ_TILE, t)
    assert n % tm == 0 and t % tt == 0 and tt % RET_CHUNK == 0 and t % min(RW_SEQ_TILE, t) == 0
    assert ffn_conv_w.shape[1] == CONV_WIDTH == 3
    vec = lambda a: a[0].reshape(1, -1).astype(F32)

    zeros = jnp.zeros((RW_DECAY_RANK, RW_WIDTH), F32)
    wa = jnp.concatenate([jnp.concatenate([rw_w2[0], zeros], axis=1),
                          jnp.concatenate([zeros, rw_a2[0]], axis=1)], axis=0).astype(BF16)

    x2 = x.reshape(n, d)
    p_rw, p_gate, y_ret = _proj_retention(x2, vec(norm_mix_pre), w_in[0].astype(BF16),
                                          positions.reshape(n, 1), tt, t // tt)

    y_rw = _rwkv_mixer(p_rw.reshape(b, t, RW_COLS), vec(rw_mu), vec(rw_w0), wa, vec(rw_a0),
                       rw_g2[0].astype(BF16), vec(rw_k_k), vec(rw_k_a), vec(rw_r_k),
                       vec(rw_lnx_w), vec(rw_lnx_b), min(RW_SEQ_TILE, t))

    h, hn = _merge(x2, y_rw.reshape(n, RW_WIDTH), y_ret, p_gate,
                   w_branch_rw[0].astype(BF16), w_branch_ret[0].astype(BF16), w_out[0].astype(BF16),
                   vec(norm_mix_post), vec(norm_ffn_pre), tm)

    out = _conv_ffn(hn.reshape(b, t, d), h.reshape(b, t, d), ffn_w_up[0].astype(BF16),
                    ffn_conv_w[0].astype(F32), vec(ffn_conv_b), ffn_w_down[0].astype(BF16),
                    vec(norm_ffn_post), tt)
    return out.astype(x.dtype)
```
